```python
import jax, jax.numpy as jnp
from jax import lax
import numpy as np

D_MODEL = 1024
BATCH = 8
SEQ = 2048
DEPTH = 1
DEC_BATCH = 128
DEC_SEQ = 4
PAST_LEN = 16384
PAGE_SIZE = 128

D_PLE = 256
CONV_WIDTH = 31
CONV_DIM = D_MODEL
HGRN_HEADS = 8
HGRN_DIM = D_MODEL
HGRN_DK = HGRN_DIM // HGRN_HEADS
HGRN_DV = HGRN_DIM // HGRN_HEADS
HGRN_CHUNK = 64
N_EXPERTS = 64
N_GROUPS = 8
TOPK_GROUPS = 4
TOP_K = 8
EXPERT_FF = 256
SHARED_FF = 256
ROUTED_SCALE = 2.5
MOE_BLOCK = 512
DEEPNORM_ALPHA = (2.0 * DEPTH) ** 0.25
DEEPNORM_BETA = (8.0 * DEPTH) ** -0.25
LN_EPS = 1e-5
SPLIT_POINTS = (CONV_DIM, 2 * CONV_DIM, 2 * CONV_DIM + HGRN_DIM, 2 * CONV_DIM + 2 * HGRN_DIM,
                2 * CONV_DIM + 3 * HGRN_DIM, 2 * CONV_DIM + 4 * HGRN_DIM, 2 * CONV_DIM + 4 * HGRN_DIM + D_MODEL)
IN_COLS = 2 * CONV_DIM + 4 * HGRN_DIM + 2 * D_MODEL

kernel_name = 'hybrid_conformer_hgrn2_moe_step'


def layer_norm(x, g, b):
    xf = x.astype(jnp.float32)
    mu = jnp.mean(xf, axis=-1, keepdims=True)
    xc = xf - mu
    var = jnp.mean(xc * xc, axis=-1, keepdims=True)
    return (xc * lax.rsqrt(var + LN_EPS) * g + b).astype(x.dtype)


def rms_norm(x, g):
    xf = x.astype(jnp.float32)
    return xf * lax.rsqrt(jnp.mean(xf * xf, axis=-1, keepdims=True) + LN_EPS) * g


def causal_depthwise_conv(u, buf, w, b):
    ext = jnp.concatenate([buf.astype(u.dtype), u], axis=1)
    y = lax.conv_general_dilated(ext, w.astype(u.dtype)[:, None, :], window_strides=(1,), padding='VALID',
                                 dimension_numbers=('NWC', 'WIO', 'NWC'), feature_group_count=CONV_DIM)
    return y + b, ext[:, ext.shape[1] - (CONV_WIDTH - 1):]


def hgrn2_recurrence(q, k, v, logf, s0):
    B, T, H, _ = q.shape
    c = min(HGRN_CHUNK, T)
    n = -(-T // c)
    pad = n * c - T

    def blocks(a):
        a = jnp.pad(a, ((0, 0), (0, pad), (0, 0), (0, 0)))
        return jnp.moveaxis(a.reshape(B, n, c, H, a.shape[-1]), 1, 0)

    causal = jnp.tril(jnp.ones((c, c), dtype=bool))

    def step(S, blk):
        qc, kc, vc, gc = blk
        L = jnp.cumsum(gc, axis=1)
        rel = L[:, :, None] - L[:, None, :]
        decay = jnp.where(causal[None, :, :, None, None], jnp.exp(jnp.minimum(rel, 0.0)), 0.0)
        scores = jnp.einsum('bthd,btshd,bshd->bths', qc, decay, kc)
        o = (jnp.einsum('bths,bshv->bthv', scores, vc)
             + jnp.einsum('bthd,bhdv->bthv', qc * jnp.exp(L), S))
        L_end = L[:, -1]
        k_end = kc * jnp.exp(L_end[:, None] - L)
        S = jnp.exp(L_end)[..., None] * S + jnp.einsum('bshd,bshv->bhdv', k_end, vc)
        return S, o

    S_fin, o = lax.scan(step, s0, (blocks(q), blocks(k), blocks(v), blocks(logf)))
    o = jnp.moveaxis(o, 0, 1).reshape(B, n * c, H, v.shape[-1])[:, :T]
    return o, S_fin


def route(x, w_router, router_bias):
    n = x.shape[0]
    s = jax.nn.sigmoid(jnp.matmul(x, w_router).astype(jnp.float32))
    sb = s + router_bias.astype(jnp.float32)
    grp = sb.reshape(n, N_GROUPS, N_EXPERTS // N_GROUPS)
    gscore = jnp.sum(lax.top_k(grp, 2)[0], axis=-1)
    _, gidx = lax.top_k(gscore, TOPK_GROUPS)
    gmask = jnp.any(gidx[:, :, None] == jnp.arange(N_GROUPS)[None, None, :], axis=1)
    emask = jnp.repeat(gmask, N_EXPERTS // N_GROUPS, axis=1)
    _, eidx = lax.top_k(jnp.where(emask, sb, -jnp.inf), TOP_K)
    w = jnp.take_along_axis(s, eidx, axis=1)
    w = w / jnp.sum(w, axis=-1, keepdims=True) * ROUTED_SCALE
    return jnp.sum(jnp.where(eidx[:, :, None] == jnp.arange(N_EXPERTS)[None, None, :], w[:, :, None], 0.0), axis=1)


def moe_ffn(x2d, w_router, router_bias, w_exp_gate, w_exp_up, w_exp_down, w_sh_gate, w_sh_up, w_sh_down):
    N, D = x2d.shape
    blk = min(MOE_BLOCK, N)
    nb = -(-N // blk)
    pad = nb * blk - N
    xb = jnp.pad(x2d, ((0, pad), (0, 0))).reshape(nb, blk, D)

    def block(xt):
        gates = route(xt, w_router, router_bias)
        h = jax.nn.silu(jnp.einsum('nd,edf->nef', xt, w_exp_gate)) * jnp.einsum('nd,edf->nef', xt, w_exp_up)
        routed = jnp.einsum('nef,efd->nd', h * gates[:, :, None].astype(h.dtype), w_exp_down)
        shared = jnp.matmul(jax.nn.silu(jnp.matmul(xt, w_sh_gate)) * jnp.matmul(xt, w_sh_up), w_sh_down)
        return (routed + shared).astype(xt.dtype)

    return lax.map(block, xb).reshape(nb * blk, D)[:N]


def layer_step(x, p, conv_buf, s0, lb, w_in, b_in, conv_w, conv_b, conv_ln_g, conv_ln_b, w_conv_out,
               hgrn_norm_g, w_hgrn_out, w_o, ln1_g, ln1_b, w_router, router_bias, w_exp_gate, w_exp_up,
               w_exp_down, w_sh_gate, w_sh_up, w_sh_down, ln2_g, ln2_b, w_ple_gate, w_ple_proj):
    B, T, D = x.shape
    z = jnp.matmul(x, w_in) + b_in
    a, a_gate, f_in, i_in, q_in, g_in, m_a, m_b = jnp.split(z, SPLIT_POINTS, axis=-1)

    u = a * jax.nn.sigmoid(a_gate)
    c, new_buf = causal_depthwise_conv(u, conv_buf, conv_w, conv_b)
    y_a = jnp.matmul(jax.nn.silu(layer_norm(c, conv_ln_g, conv_ln_b)), w_conv_out)

    f32 = f_in.astype(jnp.float32)
    f = lb + (1.0 - lb) * jax.nn.sigmoid(f32)
    k = (1.0 - lb) * jax.nn.sigmoid(-f32)
    heads = lambda t: t.reshape(B, T, HGRN_HEADS, -1)
    o, s_new = hgrn2_recurrence(heads(jax.nn.silu(q_in.astype(jnp.float32))), heads(k),
                                heads(i_in.astype(jnp.float32)), heads(jnp.log(f)), s0.astype(jnp.float32))
    o = rms_norm(o, hgrn_norm_g) * jax.nn.silu(heads(g_in.astype(jnp.float32)))
    y_b = jnp.matmul(o.reshape(B, T, HGRN_DIM).astype(x.dtype), w_hgrn_out)

    mixed = jax.nn.sigmoid(m_a) * y_a + jax.nn.sigmoid(m_b) * y_b
    x = layer_norm(DEEPNORM_ALPHA * x + jnp.matmul(mixed, w_o), ln1_g, ln1_b)

    moe_out = moe_ffn(x.reshape(B * T, D), w_router, router_bias, w_exp_gate, w_exp_up, w_exp_down,
                      w_sh_gate, w_sh_up, w_sh_down).reshape(B, T, D)
    x = layer_norm(DEEPNORM_ALPHA * x + moe_out, ln2_g, ln2_b)

    x = x + jax.nn.sigmoid(jnp.matmul(x, w_ple_gate)) * jnp.matmul(p, w_ple_proj)
    return x, new_buf, s_new.astype(s0.dtype)


def setup_inputs(seed: int = 0) -> dict:
    key = jax.random.key(seed)
    ks = jax.random.split(key, 40)
    nrm = lambda k, shape, scale: jax.random.normal(k, shape, jnp.float32) * scale
    L = DEPTH
    return {
        'x_prompt': nrm(ks[0], (BATCH, SEQ, D_MODEL), 1.0),
        'x_sample': nrm(ks[1], (DEC_BATCH, DEC_SEQ, D_MODEL), 1.0),
        'p_prompt': nrm(ks[2], (DEPTH, BATCH, SEQ, D_PLE), 1.0),
        'p_sample': nrm(ks[3], (DEPTH, DEC_BATCH, DEC_SEQ, D_PLE), 1.0),
        'state_conv': nrm(ks[4], (DEPTH, DEC_BATCH, CONV_WIDTH - 1, CONV_DIM), 0.5),
        'state_hgrn': nrm(ks[5], (DEPTH, DEC_BATCH, HGRN_HEADS, HGRN_DK, HGRN_DV), 0.5),
        'w_in': nrm(ks[6], (L, D_MODEL, IN_COLS), D_MODEL ** -0.5),
        'b_in': nrm(ks[7], (L, IN_COLS), 0.02),
        'hgrn_lb': nrm(ks[8], (L + 1, HGRN_DIM), 0.5),
        'conv_w': nrm(ks[9], (L, CONV_WIDTH, CONV_DIM), CONV_WIDTH ** -0.5),
        'conv_b': nrm(ks[10], (L, CONV_DIM), 0.02),
        'conv_ln_g': 1.0 + nrm(ks[11], (L, CONV_DIM), 0.02),
        'conv_ln_b': nrm(ks[12], (L, CONV_DIM), 0.02),
        'w_conv_out': nrm(ks[13], (L, CONV_DIM, D_MODEL), CONV_DIM ** -0.5),
        'hgrn_norm_g': 1.0 + nrm(ks[14], (L, HGRN_DV), 0.02),
        'w_hgrn_out': nrm(ks[15], (L, HGRN_DIM, D_MODEL), HGRN_DIM ** -0.5),
        'w_o': nrm(ks[16], (L, D_MODEL, D_MODEL), D_MODEL ** -0.5 * DEEPNORM_BETA),
        'ln1_g': 1.0 + nrm(ks[17], (L, D_MODEL), 0.02),
        'ln1_b': nrm(ks[18], (L, D_MODEL), 0.02),
        'w_router': nrm(ks[19], (L, D_MODEL, N_EXPERTS), D_MODEL ** -0.5),
        'router_bias': nrm(ks[20], (L, N_EXPERTS), 0.01),
        'w_exp_gate': nrm(ks[21], (L, N_EXPERTS, D_MODEL, EXPERT_FF), D_MODEL ** -0.5),
        'w_exp_up': nrm(ks[22], (L, N_EXPERTS, D_MODEL, EXPERT_FF), D_MODEL ** -0.5),
        'w_exp_down': nrm(ks[23], (L, N_EXPERTS, EXPERT_FF, D_MODEL), EXPERT_FF ** -0.5 * DEEPNORM_BETA),
        'w_sh_gate': nrm(ks[24], (L, D_MODEL, SHARED_FF), D_MODEL ** -0.5),
        'w_sh_up': nrm(ks[25], (L, D_MODEL, SHARED_FF), D_MODEL ** -0.5),
        'w_sh_down': nrm(ks[26], (L, SHARED_FF, D_MODEL), SHARED_FF ** -0.5 * DEEPNORM_BETA),
        'ln2_g': 1.0 + nrm(ks[27], (L, D_MODEL), 0.02),
        'ln2_b': nrm(ks[28], (L, D_MODEL), 0.02),
        'w_ple_gate': nrm(ks[29], (L, D_MODEL, D_MODEL), D_MODEL ** -0.5),
        'w_ple_proj': nrm(ks[30], (L, D_PLE, D_MODEL), D_PLE ** -0.5),
    }


def reference(x_prompt, x_sample, p_prompt, p_sample, state_conv, state_hgrn, w_in, b_in, hgrn_lb, conv_w,
              conv_b, conv_ln_g, conv_ln_b, w_conv_out, hgrn_norm_g, w_hgrn_out, w_o, ln1_g, ln1_b, w_router,
              router_bias, w_exp_gate, w_exp_up, w_exp_down, w_sh_gate, w_sh_up, w_sh_down, ln2_g, ln2_b,
              w_ple_gate, w_ple_proj):
    lbs = jnp.cumsum(jax.nn.softmax(hgrn_lb.astype(jnp.float32), axis=0), axis=0)
    y_prompt, y_sample = x_prompt, x_sample
    b_p = x_prompt.shape[0]
    conv_p, hgrn_p, conv_s, hgrn_s = [], [], [], []
    for i in range(DEPTH):
        wts = (lbs[i], w_in[i], b_in[i], conv_w[i], conv_b[i], conv_ln_g[i], conv_ln_b[i], w_conv_out[i],
               hgrn_norm_g[i], w_hgrn_out[i], w_o[i], ln1_g[i], ln1_b[i], w_router[i], router_bias[i],
               w_exp_gate[i], w_exp_up[i], w_exp_down[i], w_sh_gate[i], w_sh_up[i], w_sh_down[i],
               ln2_g[i], ln2_b[i], w_ple_gate[i], w_ple_proj[i])
        y_prompt, cb_p, sh_p = layer_step(
            y_prompt, p_prompt[i],
            jnp.zeros((b_p, CONV_WIDTH - 1, CONV_DIM), x_prompt.dtype),
            jnp.zeros((b_p, HGRN_HEADS, HGRN_DK, HGRN_DV), state_hgrn.dtype), *wts)
        y_sample, cb_s, sh_s = layer_step(y_sample, p_sample[i], state_conv[i], state_hgrn[i], *wts)
        conv_p.append(cb_p)
        hgrn_p.append(sh_p)
        conv_s.append(cb_s)
        hgrn_s.append(sh_s)
    new_conv_prompt = jnp.stack(conv_p, axis=0)
    new_hgrn_prompt = jnp.stack(hgrn_p, axis=0)
    new_conv_sample = jnp.stack(conv_s, axis=0)
    new_hgrn_sample = jnp.stack(hgrn_s, axis=0)
    return (y_prompt, y_sample, new_conv_prompt, new_hgrn_prompt, new_conv_sample, new_hgrn_sample)
```

```python
import functools

import jax
import jax.numpy as jnp
from jax import lax
from jax.experimental import pallas as pl
from jax.experimental.pallas import tpu as pltpu

F32 = jnp.float32
BF16 = jnp.bfloat16

D_MODEL = 1024
CONV_WIDTH = 31
HGRN_HEADS = 8
HEAD_DIM = D_MODEL // HGRN_HEADS
N_EXPERTS = 64
N_GROUPS = 8
GROUP_SIZE = N_EXPERTS // N_GROUPS
TOPK_GROUPS = 4
TOP_K = 8
EXPERT_FF = 256
ROUTED_SCALE = 2.5
LN_EPS = 1e-5
DEPTH = 1
DEEPNORM_ALPHA = (2.0 * DEPTH) ** 0.25

V7X_VMEM_BYTES = 64 * 1024 * 1024
VMEM_LIMIT = V7X_VMEM_BYTES - 8 * 1024 * 1024

TOKEN_BLOCK = 512
CONV_BLOCK = 256
CONV_ROWS = 16
CONV_HALO = 32
HGRN_CHUNK = 64
HGRN_SUB = 16
SAMPLE_PAD = 16
EXPERTS_PER_STEP = 8


def _sigmoid(x):
    return 1.0 / (1.0 + jnp.exp(-x))


def _silu(x):
    return x * _sigmoid(x)


def _layer_norm(x, g, b):
    mu = jnp.mean(x, axis=-1, keepdims=True)
    xc = x - mu
    var = jnp.mean(xc * xc, axis=-1, keepdims=True)
    return xc * lax.rsqrt(var + LN_EPS) * g + b


def _dot(a, b):
    return jnp.dot(a, b, preferred_element_type=F32)


def _dot_nt(a, b):
    return lax.dot_general(a, b, (((1,), (1,)), ((), ())), preferred_element_type=F32)


def _full(shape):
    return pl.BlockSpec(shape, lambda *_: (0,) * len(shape))


def _params(sem):
    return pltpu.CompilerParams(dimension_semantics=sem, vmem_limit_bytes=VMEM_LIMIT)


def _proj_kernel(x_ref, w_ref, b_ref, lbp_ref, u_ref, q_ref, k_ref, v_ref, lf_ref, sg_ref,
                 ga_ref, gb_ref, *, layer):
    D = D_MODEL
    xb = x_ref[...].astype(BF16)

    def col(j):
        return _dot(xb, w_ref[:, j * D:(j + 1) * D]) + b_ref[:, j * D:(j + 1) * D]

    u_ref[...] = col(0) * _sigmoid(col(1))
    hl = lbp_ref[...]
    e = jnp.exp(hl - jnp.max(hl, axis=0, keepdims=True))
    lb = jnp.sum(e[:layer + 1], axis=0, keepdims=True) / jnp.sum(e, axis=0, keepdims=True)
    fz = col(2)
    lf_ref[...] = jnp.log(lb + (1.0 - lb) * _sigmoid(fz))
    k_ref[...] = ((1.0 - lb) * _sigmoid(-fz)).astype(k_ref.dtype)
    v_ref[...] = col(3).astype(v_ref.dtype)
    q_ref[...] = _silu(col(4)).astype(q_ref.dtype)
    sg_ref[...] = _silu(col(5)).astype(sg_ref.dtype)
    ga_ref[...] = _sigmoid(col(6)).astype(ga_ref.dtype)
    gb_ref[...] = _sigmoid(col(7)).astype(gb_ref.dtype)


def _proj(x2d, w_in, b_in, hgrn_lb, layer):
    n, d = x2d.shape
    tm = TOKEN_BLOCK
    cols = w_in.shape[1]
    row = pl.BlockSpec((tm, d), lambda i: (i, 0))
    outs = [jax.ShapeDtypeStruct((n, d), dt) for dt in (F32, BF16, BF16, BF16, F32, BF16, BF16, BF16)]
    return pl.pallas_call(
        functools.partial(_proj_kernel, layer=layer),
        grid=(n // tm,),
        in_specs=[row,
                  pl.BlockSpec((d, cols), lambda i: (0, 0), pipeline_mode=pl.Buffered(1)),
                  _full((1, cols)), _full(hgrn_lb.shape)],
        out_specs=[row] * 8,
        out_shape=outs,
        compiler_params=_params(("arbitrary",)),
        name="proj",
    )(x2d, w_in, b_in, hgrn_lb)


def _conv_prompt_kernel(u_ref, cw_ref, cb_ref, g_ref, b_ref, h_ref, ext_ref):
    tc = u_ref.shape[1]

    @pl.when(pl.program_id(1) == 0)
    def _():
        ext_ref[0:CONV_HALO, :] = jnp.zeros((CONV_HALO, D_MODEL), F32)

    ext_ref[CONV_HALO:CONV_HALO + tc, :] = u_ref[0]
    shift = CONV_HALO - (CONV_WIDTH - 1)

    def body(r, carry):
        base = pl.multiple_of(r * CONV_ROWS, CONV_ROWS)
        win = ext_ref[pl.ds(base, CONV_ROWS + CONV_HALO), :]
        acc = jnp.zeros((CONV_ROWS, D_MODEL), F32) + cb_ref[...]
        for j in range(CONV_WIDTH):
            acc = acc + cw_ref[j:j + 1, :] * win[j + shift:j + shift + CONV_ROWS, :]
        y = _silu(_layer_norm(acc, g_ref[...], b_ref[...]))
        h_ref[0, pl.ds(base, CONV_ROWS), :] = y.astype(h_ref.dtype)
        return carry

    lax.fori_loop(0, tc // CONV_ROWS, body, 0)
    ext_ref[0:CONV_HALO, :] = ext_ref[tc:tc + CONV_HALO, :]


def _conv_prompt(u, conv_w, conv_b, g, b):
    bsz, t, d = u.shape
    tc = CONV_BLOCK
    blk = pl.BlockSpec((1, tc, d), lambda i, j: (i, j, 0))
    return pl.pallas_call(
        _conv_prompt_kernel,
        grid=(bsz, t // tc),
        in_specs=[blk, _full(conv_w.shape), _full((1, d)), _full((1, d)), _full((1, d))],
        out_specs=blk,
        out_shape=jax.ShapeDtypeStruct((bsz, t, d), BF16),
        scratch_shapes=[pltpu.VMEM((CONV_HALO + tc, d), F32)],
        compiler_params=_params(("arbitrary", "arbitrary")),
        name="conv_prompt",
    )(u, conv_w, conv_b, g, b)


def _conv_sample_kernel(u_ref, st_ref, cw_ref, cb_ref, g_ref, b_ref, h_ref, new_ref):
    t = u_ref.shape[1]
    nbuf = st_ref.shape[1]
    ext = jnp.concatenate([st_ref[0], u_ref[0]], axis=0)
    acc = jnp.zeros((t, D_MODEL), F32) + cb_ref[...]
    for j in range(CONV_WIDTH):
        acc = acc + cw_ref[j:j + 1, :] * ext[j:j + t, :]
    h_ref[0] = _silu(_layer_norm(acc, g_ref[...], b_ref[...]))
    new_ref[0] = ext[t:t + nbuf, :]


def _conv_sample(u, state, conv_w, conv_b, g, b):
    bsz, t, d = u.shape
    nbuf = state.shape[1]
    return pl.pallas_call(
        _conv_sample_kernel,
        grid=(bsz,),
        in_specs=[pl.BlockSpec((1, t, d), lambda i: (i, 0, 0)),
                  pl.BlockSpec((1, nbuf, d), lambda i: (i, 0, 0)),
                  _full(conv_w.shape), _full((1, d)), _full((1, d)), _full((1, d))],
        out_specs=[pl.BlockSpec((1, t, d), lambda i: (i, 0, 0)),
                   pl.BlockSpec((1, nbuf, d), lambda i: (i, 0, 0))],
        out_shape=[jax.ShapeDtypeStruct((bsz, t, d), F32),
                   jax.ShapeDtypeStruct((bsz, nbuf, d), F32)],
        compiler_params=_params(("arbitrary",)),
        name="conv_sample",
    )(u, state, conv_w, conv_b, g, b)


def _hgrn_chunk(q, k, v, g, st, consts, chunk, sub):
    tri, ones_bf, blk_mask = consts
    nb = chunk // sub
    lcum = jnp.dot(tri, g, precision=lax.Precision.HIGHEST, preferred_element_type=F32)
    l_end = lcum[chunk - 1:chunk, :]

    o = _dot_nt((q * jnp.exp(lcum)).astype(BF16), st.astype(BF16))

    if nb > 1:
        starts = [lcum[i * sub - 1:i * sub, :] for i in range(1, nb)]
        l_start = jnp.concatenate(
            [jnp.zeros((sub, HEAD_DIM), F32)]
            + [jnp.broadcast_to(s, (sub, HEAD_DIM)) for s in starts], axis=0)
        q_rel = (q * jnp.exp(lcum - l_start)).astype(BF16)
        k_stack = jnp.concatenate(
            [k[0:i * sub] * jnp.exp(starts[i - 1] - lcum[0:i * sub]) for i in range(1, nb)],
            axis=0).astype(BF16)
        v_stack = jnp.concatenate([v[0:i * sub] for i in range(1, nb)], axis=0).astype(BF16)
        scores = _dot_nt(q_rel, k_stack) * blk_mask
        o = o + _dot(scores.astype(BF16), v_stack)

    row = lax.broadcasted_iota(jnp.int32, (sub, HEAD_DIM), 0)
    diag = []
    for i in range(nb):
        sl = slice(i * sub, (i + 1) * sub)
        qi, ki, vi, li = q[sl], k[sl], v[sl], lcum[sl]
        prods = []
        for s in range(sub):
            dec = jnp.exp(jnp.minimum(li - li[s:s + 1, :], 0.0))
            prods.append(jnp.where(row >= s, qi * (ki[s:s + 1, :] * dec), 0.0))
        sums = _dot(jnp.concatenate(prods, axis=0).astype(BF16), ones_bf)
        od = sums[0:sub] * vi[0:1, :]
        for s in range(1, sub):
            od = od + sums[s * sub:(s + 1) * sub] * vi[s:s + 1, :]
        diag.append(od)
    o = o + (jnp.concatenate(diag, axis=0) if nb > 1 else diag[0])

    k_end = (k * jnp.exp(l_end - lcum)).astype(BF16)
    st_new = st * jnp.exp(l_end) + _dot(v.T.astype(BF16), k_end)
    return o, st_new


def _hgrn_consts(chunk, sub):
    nb = chunk // sub
    r = lax.broadcasted_iota(jnp.int32, (chunk, chunk), 0)
    c = lax.broadcasted_iota(jnp.int32, (chunk, chunk), 1)
    tri = jnp.where(c <= r, 1.0, 0.0).astype(F32)
    ones_bf = jnp.ones((HEAD_DIM, HEAD_DIM), BF16)
    blk_mask = None
    if nb > 1:
        width = sub * nb * (nb - 1) // 2
        rb = lax.broadcasted_iota(jnp.int32, (chunk, width), 0) // sub
        cc = lax.broadcasted_iota(jnp.int32, (chunk, width), 1)
        blk_mask = jnp.zeros((chunk, width), F32)
        off = 0
        for i in range(1, nb):
            hit = jnp.where(rb == i, jnp.where(cc >= off, jnp.where(cc < off + i * sub, 1.0, 0.0), 0.0), 0.0)
            blk_mask = blk_mask + hit
            off += i * sub
    return tri, ones_bf, blk_mask


def _hgrn_kernel(q_ref, k_ref, v_ref, lf_ref, sg_ref, gn_ref, s0_ref, o_ref, s_ref, *, chunk, sub):
    t = q_ref.shape[1]
    consts = _hgrn_consts(chunk, sub)
    gn = gn_ref[...]

    def body(n, st):
        rows = pl.ds(pl.multiple_of(n * chunk, chunk), chunk)
        o, st = _hgrn_chunk(q_ref[0, rows, :].astype(F32), k_ref[0, rows, :].astype(F32),
                            v_ref[0, rows, :].astype(F32), lf_ref[0, rows, :], st, consts, chunk, sub)
        o = o * lax.rsqrt(jnp.mean(o * o, axis=-1, keepdims=True) + LN_EPS) * gn
        o_ref[0, rows, :] = (o * sg_ref[0, rows, :].astype(F32)).astype(o_ref.dtype)
        return st

    st = lax.fori_loop(0, t // chunk, body, s0_ref[0, 0].T)
    s_ref[0, 0] = st.T


def _hgrn(q, k, v, lf, sg, gnorm, s0, chunk, sub):
    bsz, t, d = q.shape
    blk = pl.BlockSpec((1, t, HEAD_DIM), lambda b, h: (b, 0, h))
    sblk = pl.BlockSpec((1, 1, HEAD_DIM, HEAD_DIM), lambda b, h: (b, h, 0, 0))
    return pl.pallas_call(
        functools.partial(_hgrn_kernel, chunk=chunk, sub=sub),
        grid=(bsz, HGRN_HEADS),
        in_specs=[blk, blk, blk, blk, blk, _full((1, HEAD_DIM)), sblk],
        out_specs=[blk, sblk],
        out_shape=[jax.ShapeDtypeStruct((bsz, t, d), BF16),
                   jax.ShapeDtypeStruct(s0.shape, F32)],
        compiler_params=_params(("arbitrary", "arbitrary")),
        name="hgrn",
    )(q, k, v, lf, sg, gnorm, s0)


def _merge_kernel(x_ref, ha_ref, hb_ref, ga_ref, gb_ref, wa_ref, wb_ref, wo_ref, g_ref, b_ref, o_ref):
    ya = _dot(ha_ref[...].astype(BF16), wa_ref[...])
    yb = _dot(hb_ref[...].astype(BF16), wb_ref[...])
    mixed = ga_ref[...].astype(F32) * ya + gb_ref[...].astype(F32) * yb
    z = DEEPNORM_ALPHA * x_ref[...] + _dot(mixed.astype(BF16), wo_ref[...])
    o_ref[...] = _layer_norm(z, g_ref[...], b_ref[...])


def _merge(x2d, ha, hb, ga, gb, wa, wb, wo, g, b):
    n, d = x2d.shape
    tm = TOKEN_BLOCK
    row = pl.BlockSpec((tm, d), lambda i: (i, 0))
    return pl.pallas_call(
        _merge_kernel,
        grid=(n // tm,),
        in_specs=[row] * 5 + [_full((d, d))] * 3 + [_full((1, d))] * 2,
        out_specs=row,
        out_shape=jax.ShapeDtypeStruct((n, d), F32),
        compiler_params=_params(("arbitrary",)),
        name="merge",
    )(x2d, ha, hb, ga, gb, wa, wb, wo, g, b)


def _route_t(s, bias):
    n = s.shape[1]
    neg = -jnp.inf
    sb = (s + bias).reshape(N_GROUPS, GROUP_SIZE, n)
    s3 = s.reshape(N_GROUPS, GROUP_SIZE, n)
    e_in_g = lax.broadcasted_iota(jnp.int32, sb.shape, 1)
    m1 = jnp.max(sb, axis=1, keepdims=True)
    first = jnp.min(jnp.where(sb == m1, e_in_g, GROUP_SIZE), axis=1, keepdims=True)
    m2 = jnp.max(jnp.where(e_in_g == first, neg, sb), axis=1, keepdims=True)
    gscore = (m1 + m2)[:, 0, :]
    gid = lax.broadcasted_iota(jnp.int32, gscore.shape, 0)
    gsel = jnp.zeros(gscore.shape, F32)
    for _ in range(TOPK_GROUPS):
        gm = jnp.max(gscore, axis=0, keepdims=True)
        pick = jnp.min(jnp.where(gscore == gm, gid, N_GROUPS), axis=0, keepdims=True)
        hit = gid == pick
        gsel = jnp.where(hit, 1.0, gsel)
        gscore = jnp.where(hit, neg, gscore)
    cand = jnp.where(gsel[:, None, :] > 0.5, sb, neg)
    eid = lax.broadcasted_iota(jnp.int32, sb.shape, 0) * GROUP_SIZE + e_in_g
    esel = jnp.zeros(sb.shape, F32)
    for _ in range(TOP_K):
        em = jnp.max(jnp.max(cand, axis=1, keepdims=True), axis=0, keepdims=True)
        masked = jnp.where(cand == em, eid, N_EXPERTS)
        pick = jnp.min(jnp.min(masked, axis=1, keepdims=True), axis=0, keepdims=True)
        hit = eid == pick
        esel = jnp.where(hit, 1.0, esel)
        cand = jnp.where(hit, neg, cand)
    w = esel * s3
    tot = jnp.sum(jnp.sum(w, axis=1, keepdims=True), axis=0, keepdims=True)
    return (w / tot * ROUTED_SCALE).reshape(N_EXPERTS, n)


def _moe_kernel(x_ref, p_ref, wrt_ref, rb_ref, wg_ref, wu_ref, wd_ref, sg_ref, su_ref, sd_ref,
                g2_ref, b2_ref, pg_ref, pp_ref, o_ref, xb_ref, gt_ref, acc_ref):
    j = pl.program_id(1)
    eb = wg_ref.shape[0]

    @pl.when(j == 0)
    def _():
        x = x_ref[...]
        logits = lax.dot_general(wrt_ref[...], x, (((1,), (1,)), ((), ())),
                                 precision=lax.Precision.HIGHEST, preferred_element_type=F32)
        gt_ref[...] = _route_t(_sigmoid(logits), rb_ref[...])
        xb = x.astype(BF16)
        xb_ref[...] = xb
        hs = _silu(_dot(xb, sg_ref[...])) * _dot(xb, su_ref[...])
        acc_ref[...] = _dot(hs.astype(BF16), sd_ref[...])

    xb = xb_ref[...]
    gates = gt_ref[pl.ds(pl.multiple_of(j * eb, eb), eb), :].T
    for r in range(eb):
        h = _silu(_dot(xb, wg_ref[r])) * _dot(xb, wu_ref[r])
        h = h * gates[:, r:r + 1]
        acc_ref[...] += _dot(h.astype(BF16), wd_ref[r])

    @pl.when(j == pl.num_programs(1) - 1)
    def _():
        x2 = _layer_norm(DEEPNORM_ALPHA * x_ref[...] + acc_ref[...], g2_ref[...], b2_ref[...])
        gate = _sigmoid(_dot(x2.astype(BF16), pg_ref[...]))
        o_ref[...] = x2 + gate * _dot(p_ref[...].astype(BF16), pp_ref[...])


def _moe(x2d, p2d, wrt, rbias, wg, wu, wd, sg, su, sd, g2, b2, pg, pp):
    n, d = x2d.shape
    tm = TOKEN_BLOCK
    eb = EXPERTS_PER_STEP
    dp = p2d.shape[1]
    ff = wg.shape[2]
    sff = sg.shape[1]
    row = pl.BlockSpec((tm, d), lambda i, j: (i, 0))
    return pl.pallas_call(
        _moe_kernel,
        grid=(n // tm, N_EXPERTS // eb),
        in_specs=[row, pl.BlockSpec((tm, dp), lambda i, j: (i, 0)),
                  _full((N_EXPERTS, d)), _full((N_EXPERTS, 1)),
                  pl.BlockSpec((eb, d, ff), lambda i, j: (j, 0, 0)),
                  pl.BlockSpec((eb, d, ff), lambda i, j: (j, 0, 0)),
                  pl.BlockSpec((eb, ff, d), lambda i, j: (j, 0, 0)),
                  _full((d, sff)), _full((d, sff)), _full((sff, d)),
                  _full((1, d)), _full((1, d)), _full((d, d)), _full((dp, d))],
        out_specs=row,
        out_shape=jax.ShapeDtypeStruct((n, d), F32),
        scratch_shapes=[pltpu.VMEM((tm, d), BF16), pltpu.VMEM((N_EXPERTS, tm), F32),
                        pltpu.VMEM((tm, d), F32)],
        compiler_params=_params(("arbitrary", "arbitrary")),
        name="moe",
    )(x2d, p2d, wrt, rbias, wg, wu, wd, sg, su, sd, g2, b2, pg, pp)


def kernel(x_prompt, x_sample, p_prompt, p_sample, state_conv, state_hgrn, w_in, b_in, hgrn_lb, conv_w, conv_b, conv_ln_g, conv_ln_b, w_conv_out, hgrn_norm_g, w_hgrn_out, w_o, ln1_g, ln1_b, w_router, router_bias, w_exp_gate, w_exp_up, w_exp_down, w_sh_gate, w_sh_up, w_sh_down, ln2_g, ln2_b, w_ple_gate, w_ple_proj):
    assert w_in.shape[0] == DEPTH == 1
    bp, tp, d = x_prompt.shape
    bs, ts, _ = x_sample.shape
    nbuf = CONV_WIDTH - 1
    i = 0
    row = lambda a: a[i].reshape(1, -1)
    bf = lambda a: a[i].astype(BF16)

    w_in_b, wco, who, wo = bf(w_in), bf(w_conv_out), bf(w_hgrn_out), bf(w_o)
    weg, weu, wed = bf(w_exp_gate), bf(w_exp_up), bf(w_exp_down)
    wsg, wsu, wsd = bf(w_sh_gate), bf(w_sh_up), bf(w_sh_down)
    wpg, wpp = bf(w_ple_gate), bf(w_ple_proj)
    wrt = w_router[i].T
    rbias = router_bias[i].reshape(N_EXPERTS, 1)
    gnorm = row(hgrn_norm_g)

    def tail(x2d, p2d, ha, hb, ga, gb):
        x1 = _merge(x2d, ha, hb, ga, gb, wco, who, wo, row(ln1_g), row(ln1_b))
        return _moe(x1, p2d, wrt, rbias, weg, weu, wed, wsg, wsu, wsd, row(ln2_g), row(ln2_b), wpg, wpp)

    xp = x_prompt.reshape(bp * tp, d)
    u, q, k, v, lf, sg, ga, gb = _proj(xp, w_in_b, row(b_in), hgrn_lb, i)
    seq = lambda a: a.reshape(bp, tp, d)
    ha = _conv_prompt(seq(u), conv_w[i], row(conv_b), row(conv_ln_g), row(conv_ln_b))
    hb, hgrn_p = _hgrn(seq(q), seq(k), seq(v), seq(lf), seq(sg), gnorm,
                       jnp.zeros((bp, HGRN_HEADS, HEAD_DIM, HEAD_DIM), F32), HGRN_CHUNK, HGRN_SUB)
    y_p = tail(xp, p_prompt[i].reshape(bp * tp, -1), ha.reshape(bp * tp, d), hb.reshape(bp * tp, d), ga, gb)
    conv_p = seq(u)[:, tp - nbuf:, :]

    xs = x_sample.reshape(bs * ts, d)
    u, q, k, v, lf, sg, ga, gb = _proj(xs, w_in_b, row(b_in), hgrn_lb, i)
    seq = lambda a: a.reshape(bs, ts, d)
    pad = lambda a: jnp.pad(seq(a), ((0, 0), (0, SAMPLE_PAD - ts), (0, 0)))
    ha, conv_s = _conv_sample(seq(u), state_conv[i], conv_w[i], row(conv_b), row(conv_ln_g), row(conv_ln_b))
    hb, hgrn_s = _hgrn(pad(q), pad(k), pad(v), pad(lf), pad(sg), gnorm, state_hgrn[i], SAMPLE_PAD, SAMPLE_PAD)
    y_s = tail(xs, p_sample[i].reshape(bs * ts, -1), ha.reshape(bs * ts, d),
               hb[:, :ts].reshape(bs * ts, d), ga, gb)

    return (y_p.reshape(bp, tp, d), y_s.reshape(bs, ts, d), conv_p[None], hgrn_p[None],
            conv_s[None], hgrn_s[None])
```

```python
import functools

import jax
import jax.numpy as jnp
from jax import lax
from jax.experimental import pallas as pl
from jax.experimental.pallas import tpu as pltpu

F32 = jnp.float32
BF16 = jnp.bfloat16

D_MODEL = 1024
CONV_WIDTH = 31
HGRN_HEADS = 8
HEAD_DIM = D_MODEL // HGRN_HEADS
N_EXPERTS = 64
N_GROUPS = 8
GROUP_SIZE = N_EXPERTS // N_GROUPS
TOPK_GROUPS = 4
TOP_K = 8
EXPERT_FF = 256
ROUTED_SCALE = 2.5
LN_EPS = 1e-5
SUBLANES = 8
LANES = 128
DEPTH = 1
DEEPNORM_ALPHA = (2.0 * DEPTH) ** 0.25

V7X_VMEM_BYTES = 64 * 1024 * 1024
VMEM_LIMIT = V7X_VMEM_BYTES - 8 * 1024 * 1024

TOKEN_BLOCK = 512
CONV_BLOCK = 256
CONV_ROWS = 128
NORM_ROWS = 64
SAMPLE_SEQ_BLOCK = 32
CONV_HALO = 32
HGRN_BLOCK = 512
HGRN_CHUNK = 64
HGRN_SUB = 8
SAMPLE_GROUP = 16
EXPERTS_PER_STEP = 8


def _sigmoid(x):
    return 1.0 / (1.0 + jnp.exp(-x))


def _silu(x):
    return x * _sigmoid(x)


def _layer_norm(x, g, b):
    mu = jnp.mean(x, axis=-1, keepdims=True)
    xc = x - mu
    var = jnp.mean(xc * xc, axis=-1, keepdims=True)
    return xc * lax.rsqrt(var + LN_EPS) * g + b


def _dot(a, b):
    return jnp.dot(a, b, preferred_element_type=F32)


def _dot_nt(a, b):
    return lax.dot_general(a, b, (((1,), (1,)), ((), ())), preferred_element_type=F32)


def _full(shape):
    return pl.BlockSpec(shape, lambda *_: (0,) * len(shape))


def _params(sem):
    return pltpu.CompilerParams(dimension_semantics=sem, vmem_limit_bytes=VMEM_LIMIT)


def _proj_kernel(x_ref, w_ref, b_ref, lbp_ref, u_ref, q_ref, k_ref, v_ref, lf_ref, sg_ref,
                 ga_ref, gb_ref, *, layer):
    D = D_MODEL
    xb = x_ref[...].astype(BF16)

    def col(j):
        return _dot(xb, w_ref[:, j * D:(j + 1) * D]) + b_ref[:, j * D:(j + 1) * D]

    u_ref[...] = col(0) * _sigmoid(col(1))
    hl = lbp_ref[...]
    e = jnp.exp(hl - jnp.max(hl, axis=0, keepdims=True))
    lb = jnp.sum(e[:layer + 1], axis=0, keepdims=True) / jnp.sum(e, axis=0, keepdims=True)
    fz = col(2)
    lf_ref[...] = jnp.log(lb + (1.0 - lb) * _sigmoid(fz))
    k_ref[...] = ((1.0 - lb) * _sigmoid(-fz)).astype(k_ref.dtype)
    v_ref[...] = col(3).astype(v_ref.dtype)
    q_ref[...] = _silu(col(4)).astype(q_ref.dtype)
    sg_ref[...] = _silu(col(5)).astype(sg_ref.dtype)
    ga_ref[...] = _sigmoid(col(6)).astype(ga_ref.dtype)
    gb_ref[...] = _sigmoid(col(7)).astype(gb_ref.dtype)


def _proj(x2d, w_in, b_in, hgrn_lb, layer):
    n, d = x2d.shape
    tm = TOKEN_BLOCK
    cols = w_in.shape[1]
    row = pl.BlockSpec((tm, d), lambda i: (i, 0))
    outs = [jax.ShapeDtypeStruct((n, d), dt) for dt in (F32, BF16, BF16, BF16, F32, BF16, BF16, BF16)]
    return pl.pallas_call(
        functools.partial(_proj_kernel, layer=layer),
        grid=(n // tm,),
        in_specs=[row,
                  pl.BlockSpec((d, cols), lambda i: (0, 0), pipeline_mode=pl.Buffered(1)),
                  _full((1, cols)), _full(hgrn_lb.shape)],
        out_specs=[row] * 8,
        out_shape=outs,
        compiler_params=_params(("arbitrary",)),
        name="proj",
    )(x2d, w_in, b_in, hgrn_lb)


def _conv_tap_groups():
    shift = CONV_HALO - (CONV_WIDTH - 1)
    groups = [[] for _ in range(SUBLANES)]
    for j in range(CONV_WIDTH):
        groups[(j + shift) % SUBLANES].append((j, (j + shift) // SUBLANES))
    return groups


def _conv_rows(ext_ref, cw_ref, cb_ref, y_ref, base, rows):
    for l in range(D_MODEL // LANES):
        lanes = slice(l * LANES, (l + 1) * LANES)
        acc = None
        for res, taps in enumerate(_conv_tap_groups()):
            part = None
            for j, a in taps:
                term = cw_ref[j:j + 1, lanes] * ext_ref[pl.ds(base + SUBLANES * a, rows + SUBLANES), lanes]
                part = term if part is None else part + term
            part = part[res:res + rows, :]
            acc = part if acc is None else acc + part
        y_ref[pl.ds(base, rows), lanes] = acc + cb_ref[:, lanes]


def _conv_prompt_kernel(u_ref, cw_ref, cb_ref, g_ref, b_ref, h_ref, ext_ref, y_ref):
    tc = u_ref.shape[1]

    @pl.when(pl.program_id(1) == 0)
    def _():
        ext_ref[0:CONV_HALO, :] = jnp.zeros((CONV_HALO, D_MODEL), F32)
        ext_ref[CONV_HALO + tc:CONV_HALO + tc + SUBLANES, :] = jnp.zeros((SUBLANES, D_MODEL), F32)

    ext_ref[CONV_HALO:CONV_HALO + tc, :] = u_ref[0]

    def conv_body(r, carry):
        _conv_rows(ext_ref, cw_ref, cb_ref, y_ref, pl.multiple_of(r * CONV_ROWS, CONV_ROWS), CONV_ROWS)
        return carry

    lax.fori_loop(0, tc // CONV_ROWS, conv_body, 0)

    def norm_body(r, carry):
        rows = pl.ds(pl.multiple_of(r * NORM_ROWS, NORM_ROWS), NORM_ROWS)
        y = _silu(_layer_norm(y_ref[rows, :], g_ref[...], b_ref[...]))
        h_ref[0, rows, :] = y.astype(h_ref.dtype)
        return carry

    lax.fori_loop(0, tc // NORM_ROWS, norm_body, 0)
    ext_ref[0:CONV_HALO, :] = ext_ref[tc:tc + CONV_HALO, :]


def _conv_prompt(u, conv_w, conv_b, g, b):
    bsz, t, d = u.shape
    tc = CONV_BLOCK
    blk = pl.BlockSpec((1, tc, d), lambda i, j: (i, j, 0))
    return pl.pallas_call(
        _conv_prompt_kernel,
        grid=(bsz, t // tc),
        in_specs=[blk, _full(conv_w.shape), _full((1, d)), _full((1, d)), _full((1, d))],
        out_specs=blk,
        out_shape=jax.ShapeDtypeStruct((bsz, t, d), BF16),
        scratch_shapes=[pltpu.VMEM((CONV_HALO + tc + SUBLANES, d), F32), pltpu.VMEM((tc, d), F32)],
        compiler_params=_params(("arbitrary", "arbitrary")),
        name="conv_prompt",
    )(u, conv_w, conv_b, g, b)


def _conv_sample_kernel(u_ref, st_ref, cw_ref, cb_ref, g_ref, b_ref, h_ref, new_ref):
    t_new, sb, d = u_ref.shape
    nbuf = st_ref.shape[1] // d

    def ext_row(r, rows):
        if r < nbuf:
            return st_ref[rows, r * d:(r + 1) * d]
        return u_ref[r - nbuf, rows, :]

    def body(gi, carry):
        rows = pl.ds(pl.multiple_of(gi * SUBLANES, SUBLANES), SUBLANES)
        for t in range(t_new):
            acc = jnp.zeros((SUBLANES, d), F32) + cb_ref[...]
            for j in range(CONV_WIDTH):
                acc = acc + cw_ref[j:j + 1, :] * ext_row(t + j, rows)
            h_ref[t, rows, :] = _silu(_layer_norm(acc, g_ref[...], b_ref[...])).astype(h_ref.dtype)
        return carry

    lax.fori_loop(0, sb // SUBLANES, body, 0)
    new_ref[:, 0:(nbuf - t_new) * d] = st_ref[:, t_new * d:nbuf * d]
    for t in range(t_new):
        new_ref[:, (nbuf - t_new + t) * d:(nbuf - t_new + t + 1) * d] = u_ref[t]


def _conv_sample(u, state, conv_w, conv_b, g, b):
    t, bsz, d = u.shape
    nbuf = state.shape[1]
    sb = SAMPLE_SEQ_BLOCK
    st2 = state.reshape(bsz, nbuf * d)
    h, new = pl.pallas_call(
        _conv_sample_kernel,
        grid=(bsz // sb,),
        in_specs=[pl.BlockSpec((t, sb, d), lambda i: (0, i, 0)),
                  pl.BlockSpec((sb, nbuf * d), lambda i: (i, 0)),
                  _full(conv_w.shape), _full((1, d)), _full((1, d)), _full((1, d))],
        out_specs=[pl.BlockSpec((t, sb, d), lambda i: (0, i, 0)),
                   pl.BlockSpec((sb, nbuf * d), lambda i: (i, 0))],
        out_shape=[jax.ShapeDtypeStruct((t, bsz, d), F32),
                   jax.ShapeDtypeStruct((bsz, nbuf * d), F32)],
        compiler_params=_params(("arbitrary",)),
        name="conv_sample",
    )(u, st2, conv_w, conv_b, g, b)
    return h, new.reshape(bsz, nbuf, d)


def _hgrn_chunk(q, k, v, g, st, consts, chunk, sub):
    tri, ones_bf, blk_mask = consts
    nb = chunk // sub
    lcum = jnp.dot(tri, g, precision=lax.Precision.HIGHEST, preferred_element_type=F32)
    l_end = lcum[chunk - 1:chunk, :]

    o = _dot_nt((q * jnp.exp(lcum)).astype(BF16), st.astype(BF16))

    if nb > 1:
        starts = [lcum[i * sub - 1:i * sub, :] for i in range(1, nb)]
        l_start = jnp.concatenate(
            [jnp.zeros((sub, HEAD_DIM), F32)]
            + [jnp.broadcast_to(s, (sub, HEAD_DIM)) for s in starts], axis=0)
        q_rel = (q * jnp.exp(lcum - l_start)).astype(BF16)
        k_stack = jnp.concatenate(
            [k[0:i * sub] * jnp.exp(starts[i - 1] - lcum[0:i * sub]) for i in range(1, nb)],
            axis=0).astype(BF16)
        v_stack = jnp.concatenate([v[0:i * sub] for i in range(1, nb)], axis=0).astype(BF16)
        scores = _dot_nt(q_rel, k_stack) * blk_mask
        o = o + _dot(scores.astype(BF16), v_stack)

    row = lax.broadcasted_iota(jnp.int32, (sub, HEAD_DIM), 0)
    prods = []
    for i in range(nb):
        sl = slice(i * sub, (i + 1) * sub)
        qi, ki, li = q[sl], k[sl], lcum[sl]
        for s in range(sub):
            dec = jnp.exp(jnp.minimum(li - li[s:s + 1, :], 0.0))
            prods.append(jnp.where(row >= s, qi * (ki[s:s + 1, :] * dec), 0.0))
    sums = _dot(jnp.concatenate(prods, axis=0).astype(BF16), ones_bf)
    diag = []
    for i in range(nb):
        od = None
        for s in range(sub):
            r0 = (i * sub + s) * sub
            term = sums[r0:r0 + sub] * v[i * sub + s:i * sub + s + 1, :]
            od = term if od is None else od + term
        diag.append(od)
    o = o + (jnp.concatenate(diag, axis=0) if nb > 1 else diag[0])

    k_end = (k * jnp.exp(l_end - lcum)).astype(BF16)
    st_new = st * jnp.exp(l_end) + _dot(v.T.astype(BF16), k_end)
    return o, st_new


def _hgrn_consts(chunk, sub):
    nb = chunk // sub
    r = lax.broadcasted_iota(jnp.int32, (chunk, chunk), 0)
    c = lax.broadcasted_iota(jnp.int32, (chunk, chunk), 1)
    tri = jnp.where(c <= r, 1.0, 0.0).astype(F32)
    ones_bf = jnp.ones((HEAD_DIM, HEAD_DIM), BF16)
    blk_mask = None
    if nb > 1:
        width = sub * nb * (nb - 1) // 2
        rb = lax.broadcasted_iota(jnp.int32, (chunk, width), 0) // sub
        cc = lax.broadcasted_iota(jnp.int32, (chunk, width), 1)
        blk_mask = jnp.zeros((chunk, width), F32)
        off = 0
        for i in range(1, nb):
            hit = jnp.where(rb == i, jnp.where(cc >= off, jnp.where(cc < off + i * sub, 1.0, 0.0), 0.0), 0.0)
            blk_mask = blk_mask + hit
            off += i * sub
    return tri, ones_bf, blk_mask


def _head_norm(o, gn):
    return o * lax.rsqrt(jnp.mean(o * o, axis=-1, keepdims=True) + LN_EPS) * gn


def _hgrn_prompt_kernel(q_ref, k_ref, v_ref, lf_ref, sg_ref, gn_ref, s0_ref, o_ref, s_ref, st_ref, *,
                        chunk, sub):
    tb = q_ref.shape[1]
    j = pl.program_id(1)
    consts = _hgrn_consts(chunk, sub)
    gn = gn_ref[...]

    @pl.when(j == 0)
    def _():
        for h in range(HGRN_HEADS):
            st_ref[h] = s0_ref[0, h].T

    def body(n, carry):
        rows = pl.ds(pl.multiple_of(n * chunk, chunk), chunk)
        for h in range(HGRN_HEADS):
            lanes = slice(h * HEAD_DIM, (h + 1) * HEAD_DIM)
            o, st = _hgrn_chunk(q_ref[0, rows, lanes].astype(F32), k_ref[0, rows, lanes].astype(F32),
                                v_ref[0, rows, lanes].astype(F32), lf_ref[0, rows, lanes], st_ref[h],
                                consts, chunk, sub)
            st_ref[h] = st
            o_ref[0, rows, lanes] = (_head_norm(o, gn) * sg_ref[0, rows, lanes].astype(F32)).astype(o_ref.dtype)
        return carry

    lax.fori_loop(0, tb // chunk, body, 0)

    @pl.when(j == pl.num_programs(1) - 1)
    def _():
        for h in range(HGRN_HEADS):
            s_ref[0, h] = st_ref[h].T


def _hgrn_prompt(q, k, v, lf, sg, gnorm, s0):
    bsz, t, d = q.shape
    tb = HGRN_BLOCK
    blk = pl.BlockSpec((1, tb, d), lambda b, j: (b, j, 0))
    sblk = pl.BlockSpec((1, HGRN_HEADS, HEAD_DIM, HEAD_DIM), lambda b, j: (b, 0, 0, 0))
    return pl.pallas_call(
        functools.partial(_hgrn_prompt_kernel, chunk=HGRN_CHUNK, sub=HGRN_SUB),
        grid=(bsz, t // tb),
        in_specs=[blk, blk, blk, blk, blk, _full((1, HEAD_DIM)), sblk],
        out_specs=[blk, sblk],
        out_shape=[jax.ShapeDtypeStruct((bsz, t, d), BF16),
                   jax.ShapeDtypeStruct(s0.shape, F32)],
        scratch_shapes=[pltpu.VMEM((HGRN_HEADS, HEAD_DIM, HEAD_DIM), F32)],
        compiler_params=_params(("arbitrary", "arbitrary")),
        name="hgrn_prompt",
    )(q, k, v, lf, sg, gnorm, s0)


def _hgrn_sample_kernel(q_ref, k_ref, v_ref, lf_ref, sg_ref, gn_ref, s0_ref, o_ref, s_ref):
    t_new, sb, hd = q_ref.shape
    grp = SAMPLE_GROUP
    gn = gn_ref[...]

    def body(gi, carry):
        rows = pl.ds(pl.multiple_of(gi * grp, grp), grp)
        q = [q_ref[t, rows, :].astype(F32) for t in range(t_new)]
        k = [k_ref[t, rows, :].astype(F32) for t in range(t_new)]
        v = [v_ref[t, rows, :].astype(F32) for t in range(t_new)]
        lcum = []
        for t in range(t_new):
            g = lf_ref[t, rows, :]
            lcum.append(g if t == 0 else lcum[-1] + g)
        l_end = lcum[-1]
        qe = [q[t] * jnp.exp(lcum[t]) for t in range(t_new)]
        ke = [k[s] * jnp.exp(l_end - lcum[s]) for s in range(t_new)]
        f_end = jnp.exp(l_end)
        o = []
        for t in range(t_new):
            ot = None
            for s in range(t + 1):
                prod = q[t] * k[s] if s == t else q[t] * (k[s] * jnp.exp(lcum[t] - lcum[s]))
                term = jnp.sum(prod, axis=-1, keepdims=True) * v[s]
                ot = term if ot is None else ot + term
            o.append(ot)
        for d in range(hd):
            cols = slice(d * hd, (d + 1) * hd)
            s_d = s0_ref[rows, cols]
            col = lambda x: jnp.broadcast_to(x[:, d:d + 1], (grp, hd))
            for t in range(t_new):
                o[t] = o[t] + col(qe[t]) * s_d
            new = col(f_end) * s_d
            for s in range(t_new):
                new = new + col(ke[s]) * v[s]
            s_ref[rows, cols] = new
        for t in range(t_new):
            o_ref[t, rows, :] = (_head_norm(o[t], gn) * sg_ref[t, rows, :].astype(F32)).astype(o_ref.dtype)
        return carry

    lax.fori_loop(0, sb // grp, body, 0)


def _hgrn_sample(q, k, v, lf, sg, gnorm, s0):
    t, bsz, d = q.shape
    sb = SAMPLE_SEQ_BLOCK
    hh = HEAD_DIM * HEAD_DIM
    blk = pl.BlockSpec((t, sb, HEAD_DIM), lambda h, g: (0, g, h))
    sblk = pl.BlockSpec((sb, hh), lambda h, g: (g, h))
    o, s = pl.pallas_call(
        _hgrn_sample_kernel,
        grid=(HGRN_HEADS, bsz // sb),
        in_specs=[blk, blk, blk, blk, blk, _full((1, HEAD_DIM)), sblk],
        out_specs=[blk, sblk],
        out_shape=[jax.ShapeDtypeStruct((t, bsz, d), BF16),
                   jax.ShapeDtypeStruct((bsz, HGRN_HEADS * hh), F32)],
        compiler_params=_params(("arbitrary", "arbitrary")),
        name="hgrn_sample",
    )(q, k, v, lf, sg, gnorm, s0.reshape(bsz, HGRN_HEADS * hh))
    return o, s.reshape(s0.shape)


def _merge_kernel(x_ref, ha_ref, hb_ref, ga_ref, gb_ref, wa_ref, wb_ref, wo_ref, g_ref, b_ref, o_ref):
    ya = _dot(ha_ref[...].astype(BF16), wa_ref[...])
    yb = _dot(hb_ref[...].astype(BF16), wb_ref[...])
    mixed = ga_ref[...].astype(F32) * ya + gb_ref[...].astype(F32) * yb
    z = DEEPNORM_ALPHA * x_ref[...] + _dot(mixed.astype(BF16), wo_ref[...])
    o_ref[...] = _layer_norm(z, g_ref[...], b_ref[...])


def _merge(x2d, ha, hb, ga, gb, wa, wb, wo, g, b):
    n, d = x2d.shape
    tm = TOKEN_BLOCK
    row = pl.BlockSpec((tm, d), lambda i: (i, 0))
    return pl.pallas_call(
        _merge_kernel,
        grid=(n // tm,),
        in_specs=[row] * 5 + [_full((d, d))] * 3 + [_full((1, d))] * 2,
        out_specs=row,
        out_shape=jax.ShapeDtypeStruct((n, d), F32),
        compiler_params=_params(("arbitrary",)),
        name="merge",
    )(x2d, ha, hb, ga, gb, wa, wb, wo, g, b)


def _route_t(s, bias):
    n = s.shape[1]
    neg = -jnp.inf
    sb = (s + bias).reshape(N_GROUPS, GROUP_SIZE, n)
    s3 = s.reshape(N_GROUPS, GROUP_SIZE, n)
    e_in_g = lax.broadcasted_iota(jnp.int32, sb.shape, 1)
    m1 = jnp.max(sb, axis=1, keepdims=True)
    first = jnp.min(jnp.where(sb == m1, e_in_g, GROUP_SIZE), axis=1, keepdims=True)
    m2 = jnp.max(jnp.where(e_in_g == first, neg, sb), axis=1, keepdims=True)
    gscore = (m1 + m2)[:, 0, :]
    gid = lax.broadcasted_iota(jnp.int32, gscore.shape, 0)
    gsel = jnp.zeros(gscore.shape, F32)
    for _ in range(TOPK_GROUPS):
        gm = jnp.max(gscore, axis=0, keepdims=True)
        pick = jnp.min(jnp.where(gscore == gm, gid, N_GROUPS), axis=0, keepdims=True)
        hit = gid == pick
        gsel = jnp.where(hit, 1.0, gsel)
        gscore = jnp.where(hit, neg, gscore)
    cand = jnp.where(gsel[:, None, :] > 0.5, sb, neg)
    eid = lax.broadcasted_iota(jnp.int32, sb.shape, 0) * GROUP_SIZE + e_in_g
    esel = jnp.zeros(sb.shape, F32)
    for _ in range(TOP_K):
        em = jnp.max(jnp.max(cand, axis=1, keepdims=True), axis=0, keepdims=True)
        masked = jnp.where(cand == em, eid, N_EXPERTS)
        pick = jnp.min(jnp.min(masked, axis=1, keepdims=True), axis=0, keepdims=True)
        hit = eid == pick
        esel = jnp.where(hit, 1.0, esel)
        cand = jnp.where(hit, neg, cand)
    w = esel * s3
    tot = jnp.sum(jnp.sum(w, axis=1, keepdims=True), axis=0, keepdims=True)
    return (w / tot * ROUTED_SCALE).reshape(N_EXPERTS, n)


def _moe_kernel(x_ref, p_ref, wrt_ref, rb_ref, wg_ref, wu_ref, wd_ref, sg_ref, su_ref, sd_ref,
                g2_ref, b2_ref, pg_ref, pp_ref, o_ref, xb_ref, gt_ref, acc_ref):
    j = pl.program_id(1)
    eb = wg_ref.shape[0]

    @pl.when(j == 0)
    def _():
        x = x_ref[...]
        logits = lax.dot_general(wrt_ref[...], x, (((1,), (1,)), ((), ())),
                                 precision=lax.Precision.HIGHEST, preferred_element_type=F32)
        gt_ref[...] = _route_t(_sigmoid(logits), rb_ref[...])
        xb = x.astype(BF16)
        xb_ref[...] = xb
        hs = _silu(_dot(xb, sg_ref[...])) * _dot(xb, su_ref[...])
        acc_ref[...] = _dot(hs.astype(BF16), sd_ref[...])

    xb = xb_ref[...]
    gates = gt_ref[pl.ds(pl.multiple_of(j * eb, eb), eb), :].T
    for r in range(eb):
        h = _silu(_dot(xb, wg_ref[r])) * _dot(xb, wu_ref[r])
        h = h * gates[:, r:r + 1]
        acc_ref[...] += _dot(h.astype(BF16), wd_ref[r])

    @pl.when(j == pl.num_programs(1) - 1)
    def _():
        x2 = _layer_norm(DEEPNORM_ALPHA * x_ref[...] + acc_ref[...], g2_ref[...], b2_ref[...])
        gate = _sigmoid(_dot(x2.astype(BF16), pg_ref[...]))
        o_ref[...] = x2 + gate * _dot(p_ref[...].astype(BF16), pp_ref[...])


def _moe(x2d, p2d, wrt, rbias, wg, wu, wd, sg, su, sd, g2, b2, pg, pp):
    n, d = x2d.shape
    tm = TOKEN_BLOCK
    eb = EXPERTS_PER_STEP
    dp = p2d.shape[1]
    ff = wg.shape[2]
    sff = sg.shape[1]
    row = pl.BlockSpec((tm, d), lambda i, j: (i, 0))
    return pl.pallas_call(
        _moe_kernel,
        grid=(n // tm, N_EXPERTS // eb),
        in_specs=[row, pl.BlockSpec((tm, dp), lambda i, j: (i, 0)),
                  _full((N_EXPERTS, d)), _full((N_EXPERTS, 1)),
                  pl.BlockSpec((eb, d, ff), lambda i, j: (j, 0, 0)),
                  pl.BlockSpec((eb, d, ff), lambda i, j: (j, 0, 0)),
                  pl.BlockSpec((eb, ff, d), lambda i, j: (j, 0, 0)),
                  _full((d, sff)), _full((d, sff)), _full((sff, d)),
                  _full((1, d)), _full((1, d)), _full((d, d)), _full((dp, d))],
        out_specs=row,
        out_shape=jax.ShapeDtypeStruct((n, d), F32),
        scratch_shapes=[pltpu.VMEM((tm, d), BF16), pltpu.VMEM((N_EXPERTS, tm), F32),
                        pltpu.VMEM((tm, d), F32)],
        compiler_params=_params(("arbitrary", "arbitrary")),
        name="moe",
    )(x2d, p2d, wrt, rbias, wg, wu, wd, sg, su, sd, g2, b2, pg, pp)


def kernel(x_prompt, x_sample, p_prompt, p_sample, state_conv, state_hgrn, w_in, b_in, hgrn_lb, conv_w, conv_b, conv_ln_g, conv_ln_b, w_conv_out, hgrn_norm_g, w_hgrn_out, w_o, ln1_g, ln1_b, w_router, router_bias, w_exp_gate, w_exp_up, w_exp_down, w_sh_gate, w_sh_up, w_sh_down, ln2_g, ln2_b, w_ple_gate, w_ple_proj):
    assert w_in.shape[0] == DEPTH == 1
    bp, tp, d = x_prompt.shape
    bs, ts, _ = x_sample.shape
    nbuf = CONV_WIDTH - 1
    i = 0
    row = lambda a: a[i].reshape(1, -1)
    bf = lambda a: a[i].astype(BF16)

    w_in_b, wco, who, wo = bf(w_in), bf(w_conv_out), bf(w_hgrn_out), bf(w_o)
    weg, weu, wed = bf(w_exp_gate), bf(w_exp_up), bf(w_exp_down)
    wsg, wsu, wsd = bf(w_sh_gate), bf(w_sh_up), bf(w_sh_down)
    wpg, wpp = bf(w_ple_gate), bf(w_ple_proj)
    wrt = w_router[i].T
    rbias = router_bias[i].reshape(N_EXPERTS, 1)
    gnorm = row(hgrn_norm_g)

    def tail(x2d, p2d, ha, hb, ga, gb):
        x1 = _merge(x2d, ha, hb, ga, gb, wco, who, wo, row(ln1_g), row(ln1_b))
        return _moe(x1, p2d, wrt, rbias, weg, weu, wed, wsg, wsu, wsd, row(ln2_g), row(ln2_b), wpg, wpp)

    xp = x_prompt.reshape(bp * tp, d)
    u, q, k, v, lf, sg, ga, gb = _proj(xp, w_in_b, row(b_in), hgrn_lb, i)
    seq = lambda a: a.reshape(bp, tp, d)
    ha = _conv_prompt(seq(u), conv_w[i], row(conv_b), row(conv_ln_g), row(conv_ln_b))
    hb, hgrn_p = _hgrn_prompt(seq(q), seq(k), seq(v), seq(lf), seq(sg), gnorm,
                              jnp.zeros((bp, HGRN_HEADS, HEAD_DIM, HEAD_DIM), F32))
    y_p = tail(xp, p_prompt[i].reshape(bp * tp, -1), ha.reshape(bp * tp, d), hb.reshape(bp * tp, d), ga, gb)
    conv_p = seq(u)[:, tp - nbuf:, :]

    tmaj = lambda a: jnp.swapaxes(a, 0, 1).reshape(ts * bs, -1)
    xs = tmaj(x_sample)
    u, q, k, v, lf, sg, ga, gb = _proj(xs, w_in_b, row(b_in), hgrn_lb, i)
    seq = lambda a: a.reshape(ts, bs, d)
    ha, conv_s = _conv_sample(seq(u), state_conv[i], conv_w[i], row(conv_b), row(conv_ln_g), row(conv_ln_b))
    hb, hgrn_s = _hgrn_sample(seq(q), seq(k), seq(v), seq(lf), seq(sg), gnorm, state_hgrn[i])
    y_s = tail(xs, tmaj(p_sample[i]), ha.reshape(ts * bs, d), hb.reshape(ts * bs, d), ga, gb)
    y_s = jnp.swapaxes(y_s.reshape(ts, bs, d), 0, 1)

    return (y_p.reshape(bp, tp, d), y_s, conv_p[None], hgrn_p[None], conv_s[None], hgrn_s[None])
```

```python
import functools

import jax
import jax.numpy as jnp
from jax import lax
from jax.experimental import pallas as pl
from jax.experimental.pallas import tpu as pltpu

F32 = jnp.float32
BF16 = jnp.bfloat16

D_MODEL = 1024
CONV_WIDTH = 31
HGRN_HEADS = 8
HEAD_DIM = D_MODEL // HGRN_HEADS
N_EXPERTS = 64
N_GROUPS = 8
GROUP_SIZE = N_EXPERTS // N_GROUPS
TOPK_GROUPS = 4
TOP_K = 8
EXPERT_FF = 256
ROUTED_SCALE = 2.5
LN_EPS = 1e-5
LOG2_E = 1.4426950408889634
NEG_BIG = 1e30
SUBLANES = 8
LANES = 128
DEPTH = 1
DEEPNORM_ALPHA = (2.0 * DEPTH) ** 0.25

V7X_VMEM_BYTES = 64 * 1024 * 1024
VMEM_LIMIT = V7X_VMEM_BYTES - 8 * 1024 * 1024

TOKEN_BLOCK = 512
CONV_BLOCK = 256
CONV_ROWS = 128
NORM_ROWS = 64
SAMPLE_SEQ_BLOCK = 32
CONV_HALO = 32
HGRN_BLOCK = 512
HGRN_HEAD_GROUP = 8
HGRN_CHUNK = 64
HGRN_SUB = 8
SAMPLE_GROUP = 16
MOE_TOKEN_BLOCK = 1024
EXPERTS_PER_STEP = 4


def _sigmoid(x):
    return 1.0 / (1.0 + jnp.exp(-x))


def _silu(x):
    return x * _sigmoid(x)


def _layer_norm(x, g, b):
    mu = jnp.mean(x, axis=-1, keepdims=True)
    xc = x - mu
    var = jnp.mean(xc * xc, axis=-1, keepdims=True)
    return xc * lax.rsqrt(var + LN_EPS) * g + b


def _dot(a, b):
    return jnp.dot(a, b, preferred_element_type=F32)


def _dot_nt(a, b):
    return lax.dot_general(a, b, (((1,), (1,)), ((), ())), preferred_element_type=F32)


def _full(shape):
    return pl.BlockSpec(shape, lambda *_: (0,) * len(shape))


def _params(sem):
    return pltpu.CompilerParams(dimension_semantics=sem, vmem_limit_bytes=VMEM_LIMIT)


def _proj_kernel(x_ref, w_ref, b_ref, lbp_ref, u_ref, q_ref, k_ref, v_ref, lf_ref, sg_ref,
                 ga_ref, gb_ref, *, layer):
    D = D_MODEL
    xb = x_ref[...].astype(BF16)

    def col(j):
        return _dot(xb, w_ref[:, j * D:(j + 1) * D]) + b_ref[:, j * D:(j + 1) * D]

    u_ref[...] = col(0) * _sigmoid(col(1))
    hl = lbp_ref[...]
    e = jnp.exp(hl - jnp.max(hl, axis=0, keepdims=True))
    lb = jnp.sum(e[:layer + 1], axis=0, keepdims=True) / jnp.sum(e, axis=0, keepdims=True)
    fz = col(2)
    lf_ref[...] = jnp.log(lb + (1.0 - lb) * _sigmoid(fz))
    k_ref[...] = ((1.0 - lb) * _sigmoid(-fz)).astype(k_ref.dtype)
    v_ref[...] = col(3).astype(v_ref.dtype)
    q_ref[...] = _silu(col(4)).astype(q_ref.dtype)
    sg_ref[...] = _silu(col(5)).astype(sg_ref.dtype)
    ga_ref[...] = _sigmoid(col(6)).astype(ga_ref.dtype)
    gb_ref[...] = _sigmoid(col(7)).astype(gb_ref.dtype)


def _proj(x2d, w_in, b_in, hgrn_lb, layer):
    n, d = x2d.shape
    tm = TOKEN_BLOCK
    cols = w_in.shape[1]
    row = pl.BlockSpec((tm, d), lambda i: (i, 0))
    outs = [jax.ShapeDtypeStruct((n, d), dt) for dt in (F32, BF16, BF16, BF16, F32, BF16, BF16, BF16)]
    return pl.pallas_call(
        functools.partial(_proj_kernel, layer=layer),
        grid=(n // tm,),
        in_specs=[row,
                  pl.BlockSpec((d, cols), lambda i: (0, 0), pipeline_mode=pl.Buffered(1)),
                  _full((1, cols)), _full(hgrn_lb.shape)],
        out_specs=[row] * 8,
        out_shape=outs,
        compiler_params=_params(("arbitrary",)),
        name="proj",
    )(x2d, w_in, b_in, hgrn_lb)


def _conv_tap_groups():
    shift = CONV_HALO - (CONV_WIDTH - 1)
    groups = [[] for _ in range(SUBLANES)]
    for j in range(CONV_WIDTH):
        groups[(j + shift) % SUBLANES].append((j, (j + shift) // SUBLANES))
    return groups


def _conv_rows(ext_ref, cw_ref, cb_ref, y_ref, base, rows):
    for l in range(D_MODEL // LANES):
        lanes = slice(l * LANES, (l + 1) * LANES)
        acc = None
        for res, taps in enumerate(_conv_tap_groups()):
            part = None
            for j, a in taps:
                term = cw_ref[j:j + 1, lanes] * ext_ref[pl.ds(base + SUBLANES * a, rows + SUBLANES), lanes]
                part = term if part is None else part + term
            part = part[res:res + rows, :]
            acc = part if acc is None else acc + part
        y_ref[pl.ds(base, rows), lanes] = acc + cb_ref[:, lanes]


def _conv_prompt_kernel(u_ref, cw_ref, cb_ref, g_ref, b_ref, h_ref, ext_ref, y_ref):
    tc = u_ref.shape[1]

    @pl.when(pl.program_id(1) == 0)
    def _():
        ext_ref[0:CONV_HALO, :] = jnp.zeros((CONV_HALO, D_MODEL), F32)
        ext_ref[CONV_HALO + tc:CONV_HALO + tc + SUBLANES, :] = jnp.zeros((SUBLANES, D_MODEL), F32)

    ext_ref[CONV_HALO:CONV_HALO + tc, :] = u_ref[0]

    def conv_body(r, carry):
        _conv_rows(ext_ref, cw_ref, cb_ref, y_ref, pl.multiple_of(r * CONV_ROWS, CONV_ROWS), CONV_ROWS)
        return carry

    lax.fori_loop(0, tc // CONV_ROWS, conv_body, 0)

    def norm_body(r, carry):
        rows = pl.ds(pl.multiple_of(r * NORM_ROWS, NORM_ROWS), NORM_ROWS)
        y = _silu(_layer_norm(y_ref[rows, :], g_ref[...], b_ref[...]))
        h_ref[0, rows, :] = y.astype(h_ref.dtype)
        return carry

    lax.fori_loop(0, tc // NORM_ROWS, norm_body, 0)
    ext_ref[0:CONV_HALO, :] = ext_ref[tc:tc + CONV_HALO, :]


def _conv_prompt(u, conv_w, conv_b, g, b):
    bsz, t, d = u.shape
    tc = CONV_BLOCK
    blk = pl.BlockSpec((1, tc, d), lambda i, j: (i, j, 0))
    return pl.pallas_call(
        _conv_prompt_kernel,
        grid=(bsz, t // tc),
        in_specs=[blk, _full(conv_w.shape), _full((1, d)), _full((1, d)), _full((1, d))],
        out_specs=blk,
        out_shape=jax.ShapeDtypeStruct((bsz, t, d), BF16),
        scratch_shapes=[pltpu.VMEM((CONV_HALO + tc + SUBLANES, d), F32), pltpu.VMEM((tc, d), F32)],
        compiler_params=_params(("arbitrary", "arbitrary")),
        name="conv_prompt",
    )(u, conv_w, conv_b, g, b)


def _conv_sample_kernel(u_ref, st_ref, cw_ref, cb_ref, g_ref, b_ref, h_ref, new_ref):
    t_new, sb, d = u_ref.shape
    nbuf = st_ref.shape[1] // d

    def ext_row(r, rows):
        if r < nbuf:
            return st_ref[rows, r * d:(r + 1) * d]
        return u_ref[r - nbuf, rows, :]

    def body(gi, carry):
        rows = pl.ds(pl.multiple_of(gi * SUBLANES, SUBLANES), SUBLANES)
        for t in range(t_new):
            acc = jnp.zeros((SUBLANES, d), F32) + cb_ref[...]
            for j in range(CONV_WIDTH):
                acc = acc + cw_ref[j:j + 1, :] * ext_row(t + j, rows)
            h_ref[t, rows, :] = _silu(_layer_norm(acc, g_ref[...], b_ref[...])).astype(h_ref.dtype)
        return carry

    lax.fori_loop(0, sb // SUBLANES, body, 0)
    new_ref[:, 0:(nbuf - t_new) * d] = st_ref[:, t_new * d:nbuf * d]
    for t in range(t_new):
        new_ref[:, (nbuf - t_new + t) * d:(nbuf - t_new + t + 1) * d] = u_ref[t]


def _conv_sample(u, state, conv_w, conv_b, g, b):
    t, bsz, d = u.shape
    nbuf = state.shape[1]
    sb = SAMPLE_SEQ_BLOCK
    st2 = state.reshape(bsz, nbuf * d)
    h, new = pl.pallas_call(
        _conv_sample_kernel,
        grid=(bsz // sb,),
        in_specs=[pl.BlockSpec((t, sb, d), lambda i: (0, i, 0)),
                  pl.BlockSpec((sb, nbuf * d), lambda i: (i, 0)),
                  _full(conv_w.shape), _full((1, d)), _full((1, d)), _full((1, d))],
        out_specs=[pl.BlockSpec((t, sb, d), lambda i: (0, i, 0)),
                   pl.BlockSpec((sb, nbuf * d), lambda i: (i, 0))],
        out_shape=[jax.ShapeDtypeStruct((t, bsz, d), F32),
                   jax.ShapeDtypeStruct((bsz, nbuf * d), F32)],
        compiler_params=_params(("arbitrary",)),
        name="conv_sample",
    )(u, st2, conv_w, conv_b, g, b)
    return h, new.reshape(bsz, nbuf, d)


def _hgrn_chunk(q, k, v, g, st, consts, chunk, sub):
    _, blk_mask, neg_masks = consts
    nb = chunk // sub
    lcum = g
    l_end = lcum[chunk - 1:chunk, :]

    o = _dot_nt((q * jnp.exp2(lcum)).astype(BF16), st.astype(BF16))

    if nb > 1:
        starts = [lcum[i * sub - 1:i * sub, :] for i in range(1, nb)]
        l_start = jnp.concatenate(
            [jnp.zeros((sub, HEAD_DIM), F32)]
            + [jnp.broadcast_to(s, (sub, HEAD_DIM)) for s in starts], axis=0)
        q_rel = (q * jnp.exp2(lcum - l_start)).astype(BF16)
        k_stack = jnp.concatenate(
            [k[0:i * sub] * jnp.exp2(starts[i - 1] - lcum[0:i * sub]) for i in range(1, nb)],
            axis=0).astype(BF16)
        v_stack = jnp.concatenate([v[0:i * sub] for i in range(1, nb)], axis=0).astype(BF16)
        scores = _dot_nt(q_rel, k_stack) * blk_mask
        o = o + _dot(scores.astype(BF16), v_stack)

    diag = []
    for i in range(nb):
        sl = slice(i * sub, (i + 1) * sub)
        qi, ki, li = q[sl], k[sl], lcum[sl]
        od = None
        for s in range(sub):
            dec = jnp.exp2(li - li[s:s + 1, :] + neg_masks[s])
            score = jnp.sum(qi * (ki[s:s + 1, :] * dec), axis=-1, keepdims=True)
            term = score * v[i * sub + s:i * sub + s + 1, :]
            od = term if od is None else od + term
        diag.append(od)
    o = o + (jnp.concatenate(diag, axis=0) if nb > 1 else diag[0])

    k_end = (k * jnp.exp2(l_end - lcum)).astype(BF16)
    st_new = st * jnp.exp2(l_end) + _dot(v.T.astype(BF16), k_end)
    return o, st_new


def _hgrn_consts(chunk, sub):
    nb = chunk // sub
    r = lax.broadcasted_iota(jnp.int32, (chunk, chunk), 0)
    c = lax.broadcasted_iota(jnp.int32, (chunk, chunk), 1)
    tri = jnp.where(c <= r, 1.0, 0.0).astype(BF16)
    row = lax.broadcasted_iota(jnp.int32, (sub, HEAD_DIM), 0)
    neg_masks = [jnp.where(row >= s, 0.0, -NEG_BIG).astype(F32) for s in range(sub)]
    blk_mask = None
    if nb > 1:
        width = sub * nb * (nb - 1) // 2
        rb = lax.broadcasted_iota(jnp.int32, (chunk, width), 0) // sub
        cc = lax.broadcasted_iota(jnp.int32, (chunk, width), 1)
        blk_mask = jnp.zeros((chunk, width), F32)
        off = 0
        for i in range(1, nb):
            hit = jnp.where(rb == i, jnp.where(cc >= off, jnp.where(cc < off + i * sub, 1.0, 0.0), 0.0), 0.0)
            blk_mask = blk_mask + hit
            off += i * sub
    return tri, blk_mask, neg_masks


def _cumsum_rows(tri_bf, g):
    g1 = g.astype(BF16)
    r1 = g - g1.astype(F32)
    g2 = r1.astype(BF16)
    g3 = (r1 - g2.astype(F32)).astype(BF16)
    return _dot(tri_bf, g1) + _dot(tri_bf, g2) + _dot(tri_bf, g3)


def _head_norm(o, gn):
    return o * lax.rsqrt(jnp.mean(o * o, axis=-1, keepdims=True) + LN_EPS) * gn


def _hgrn_prompt_kernel(q_ref, k_ref, v_ref, lf_ref, sg_ref, gn_ref, s0_ref, o_ref, s_ref, st_ref, *,
                        chunk, sub):
    tb = q_ref.shape[1]
    heads = q_ref.shape[2] // HEAD_DIM
    j = pl.program_id(2)
    consts = _hgrn_consts(chunk, sub)
    gn = gn_ref[...]

    @pl.when(j == 0)
    def _():
        for h in range(heads):
            st_ref[h] = s0_ref[0, h].T

    def body(n, carry):
        rows = pl.ds(pl.multiple_of(n * chunk, chunk), chunk)
        lcum = _cumsum_rows(consts[0], lf_ref[0, rows, :]) * LOG2_E
        for h in range(heads):
            lanes = slice(h * HEAD_DIM, (h + 1) * HEAD_DIM)
            o, st = _hgrn_chunk(q_ref[0, rows, lanes].astype(F32), k_ref[0, rows, lanes].astype(F32),
                                v_ref[0, rows, lanes].astype(F32), lcum[:, lanes], st_ref[h],
                                consts, chunk, sub)
            st_ref[h] = st
            o_ref[0, rows, lanes] = (_head_norm(o, gn) * sg_ref[0, rows, lanes].astype(F32)).astype(o_ref.dtype)
        return carry

    lax.fori_loop(0, tb // chunk, body, 0)

    @pl.when(j == pl.num_programs(2) - 1)
    def _():
        for h in range(heads):
            s_ref[0, h] = st_ref[h].T


def _hgrn_prompt(q, k, v, lf, sg, gnorm, s0):
    bsz, t, d = q.shape
    tb = HGRN_BLOCK
    hg = HGRN_HEAD_GROUP
    blk = pl.BlockSpec((1, tb, hg * HEAD_DIM), lambda b, g, j: (b, j, g))
    sblk = pl.BlockSpec((1, hg, HEAD_DIM, HEAD_DIM), lambda b, g, j: (b, g, 0, 0))
    return pl.pallas_call(
        functools.partial(_hgrn_prompt_kernel, chunk=HGRN_CHUNK, sub=HGRN_SUB),
        grid=(bsz, HGRN_HEADS // hg, t // tb),
        in_specs=[blk, blk, blk, blk, blk, _full((1, HEAD_DIM)), sblk],
        out_specs=[blk, sblk],
        out_shape=[jax.ShapeDtypeStruct((bsz, t, d), BF16),
                   jax.ShapeDtypeStruct(s0.shape, F32)],
        scratch_shapes=[pltpu.VMEM((hg, HEAD_DIM, HEAD_DIM), F32)],
        compiler_params=_params(("arbitrary", "arbitrary", "arbitrary")),
        name="hgrn_prompt",
    )(q, k, v, lf, sg, gnorm, s0)


def _hgrn_sample_kernel(q_ref, k_ref, v_ref, lf_ref, sg_ref, gn_ref, s0_ref, o_ref, s_ref):
    t_new, sb, hd = q_ref.shape
    grp = SAMPLE_GROUP
    gn = gn_ref[...]

    def body(gi, carry):
        rows = pl.ds(pl.multiple_of(gi * grp, grp), grp)
        q = [q_ref[t, rows, :].astype(F32) for t in range(t_new)]
        k = [k_ref[t, rows, :].astype(F32) for t in range(t_new)]
        v = [v_ref[t, rows, :].astype(F32) for t in range(t_new)]
        lcum = []
        for t in range(t_new):
            g = lf_ref[t, rows, :]
            lcum.append(g if t == 0 else lcum[-1] + g)
        l_end = lcum[-1]
        qe = [q[t] * jnp.exp(lcum[t]) for t in range(t_new)]
        ke = [k[s] * jnp.exp(l_end - lcum[s]) for s in range(t_new)]
        f_end = jnp.exp(l_end)
        o = []
        for t in range(t_new):
            ot = None
            for s in range(t + 1):
                prod = q[t] * k[s] if s == t else q[t] * (k[s] * jnp.exp(lcum[t] - lcum[s]))
                term = jnp.sum(prod, axis=-1, keepdims=True) * v[s]
                ot = term if ot is None else ot + term
            o.append(ot)
        for d in range(hd):
            cols = slice(d * hd, (d + 1) * hd)
            s_d = s0_ref[rows, cols]
            col = lambda x: jnp.broadcast_to(x[:, d:d + 1], (grp, hd))
            for t in range(t_new):
                o[t] = o[t] + col(qe[t]) * s_d
            new = col(f_end) * s_d
            for s in range(t_new):
                new = new + col(ke[s]) * v[s]
            s_ref[rows, cols] = new
        for t in range(t_new):
            o_ref[t, rows, :] = (_head_norm(o[t], gn) * sg_ref[t, rows, :].astype(F32)).astype(o_ref.dtype)
        return carry

    lax.fori_loop(0, sb // grp, body, 0)


def _hgrn_sample(q, k, v, lf, sg, gnorm, s0):
    t, bsz, d = q.shape
    sb = SAMPLE_SEQ_BLOCK
    hh = HEAD_DIM * HEAD_DIM
    blk = pl.BlockSpec((t, sb, HEAD_DIM), lambda h, g: (0, g, h))
    sblk = pl.BlockSpec((sb, hh), lambda h, g: (g, h))
    o, s = pl.pallas_call(
        _hgrn_sample_kernel,
        grid=(HGRN_HEADS, bsz // sb),
        in_specs=[blk, blk, blk, blk, blk, _full((1, HEAD_DIM)), sblk],
        out_specs=[blk, sblk],
        out_shape=[jax.ShapeDtypeStruct((t, bsz, d), BF16),
                   jax.ShapeDtypeStruct((bsz, HGRN_HEADS * hh), F32)],
        compiler_params=_params(("arbitrary", "arbitrary")),
        name="hgrn_sample",
    )(q, k, v, lf, sg, gnorm, s0.reshape(bsz, HGRN_HEADS * hh))
    return o, s.reshape(s0.shape)


def _merge_kernel(x_ref, ha_ref, hb_ref, ga_ref, gb_ref, wa_ref, wb_ref, wo_ref, g_ref, b_ref, o_ref):
    ya = _dot(ha_ref[...].astype(BF16), wa_ref[...])
    yb = _dot(hb_ref[...].astype(BF16), wb_ref[...])
    mixed = ga_ref[...].astype(F32) * ya + gb_ref[...].astype(F32) * yb
    z = DEEPNORM_ALPHA * x_ref[...] + _dot(mixed.astype(BF16), wo_ref[...])
    o_ref[...] = _layer_norm(z, g_ref[...], b_ref[...])


def _merge(x2d, ha, hb, ga, gb, wa, wb, wo, g, b):
    n, d = x2d.shape
    tm = TOKEN_BLOCK
    row = pl.BlockSpec((tm, d), lambda i: (i, 0))
    return pl.pallas_call(
        _merge_kernel,
        grid=(n // tm,),
        in_specs=[row] * 5 + [_full((d, d))] * 3 + [_full((1, d))] * 2,
        out_specs=row,
        out_shape=jax.ShapeDtypeStruct((n, d), F32),
        compiler_params=_params(("arbitrary",)),
        name="merge",
    )(x2d, ha, hb, ga, gb, wa, wb, wo, g, b)


def _route_t(s, bias):
    n = s.shape[1]
    neg = -jnp.inf
    sb = (s + bias).reshape(N_GROUPS, GROUP_SIZE, n)
    s3 = s.reshape(N_GROUPS, GROUP_SIZE, n)
    e_in_g = lax.broadcasted_iota(jnp.int32, sb.shape, 1)
    m1 = jnp.max(sb, axis=1, keepdims=True)
    first = jnp.min(jnp.where(sb == m1, e_in_g, GROUP_SIZE), axis=1, keepdims=True)
    m2 = jnp.max(jnp.where(e_in_g == first, neg, sb), axis=1, keepdims=True)
    gscore = (m1 + m2)[:, 0, :]
    gid = lax.broadcasted_iota(jnp.int32, gscore.shape, 0)
    gsel = jnp.zeros(gscore.shape, F32)
    for _ in range(TOPK_GROUPS):
        gm = jnp.max(gscore, axis=0, keepdims=True)
        pick = jnp.min(jnp.where(gscore == gm, gid, N_GROUPS), axis=0, keepdims=True)
        hit = gid == pick
        gsel = jnp.where(hit, 1.0, gsel)
        gscore = jnp.where(hit, neg, gscore)
    cand = jnp.where(gsel[:, None, :] > 0.5, sb, neg)
    eid = lax.broadcasted_iota(jnp.int32, sb.shape, 0) * GROUP_SIZE + e_in_g
    esel = jnp.zeros(sb.shape, F32)
    for _ in range(TOP_K):
        em = jnp.max(jnp.max(cand, axis=1, keepdims=True), axis=0, keepdims=True)
        masked = jnp.where(cand == em, eid, N_EXPERTS)
        pick = jnp.min(jnp.min(masked, axis=1, keepdims=True), axis=0, keepdims=True)
        hit = eid == pick
        esel = jnp.where(hit, 1.0, esel)
        cand = jnp.where(hit, neg, cand)
    w = esel * s3
    tot = jnp.sum(jnp.sum(w, axis=1, keepdims=True), axis=0, keepdims=True)
    return (w / tot * ROUTED_SCALE).reshape(N_EXPERTS, n)


def _moe_kernel(x_ref, p_ref, wrt_ref, rb_ref, wg_ref, wu_ref, wd_ref, sg_ref, su_ref, sd_ref,
                g2_ref, b2_ref, pg_ref, pp_ref, o_ref, xb_ref, gt_ref, acc_ref):
    j = pl.program_id(1)
    eb = wg_ref.shape[0]

    @pl.when(j == 0)
    def _():
        x = x_ref[...]
        logits = lax.dot_general(wrt_ref[...], x, (((1,), (1,)), ((), ())),
                                 precision=lax.Precision.HIGHEST, preferred_element_type=F32)
        gt_ref[...] = _route_t(_sigmoid(logits), rb_ref[...])
        xb = x.astype(BF16)
        xb_ref[...] = xb
        hs = _silu(_dot(xb, sg_ref[...])) * _dot(xb, su_ref[...])
        acc_ref[...] = _dot(hs.astype(BF16), sd_ref[...])

    xb = xb_ref[...]
    first = j * eb
    slab = lax.shift_right_logical(first, 3) * SUBLANES
    gates = gt_ref[pl.ds(pl.multiple_of(slab, SUBLANES), SUBLANES), :].T
    in_slab = first - slab
    for r in range(eb):
        gate = gates[:, r:r + 1]
        for off in range(eb, SUBLANES, eb):
            gate = jnp.where(in_slab == off, gates[:, off + r:off + r + 1], gate)
        h = _silu(_dot(xb, wg_ref[r])) * _dot(xb, wu_ref[r])
        acc_ref[...] += _dot((h * gate).astype(BF16), wd_ref[r])

    @pl.when(j == pl.num_programs(1) - 1)
    def _():
        x2 = _layer_norm(DEEPNORM_ALPHA * x_ref[...] + acc_ref[...], g2_ref[...], b2_ref[...])
        gate = _sigmoid(_dot(x2.astype(BF16), pg_ref[...]))
        o_ref[...] = x2 + gate * _dot(p_ref[...].astype(BF16), pp_ref[...])


def _moe(x2d, p2d, wrt, rbias, wg, wu, wd, sg, su, sd, g2, b2, pg, pp):
    n, d = x2d.shape
    tm = MOE_TOKEN_BLOCK if n % MOE_TOKEN_BLOCK == 0 else TOKEN_BLOCK
    eb = EXPERTS_PER_STEP
    assert SUBLANES % eb == 0 and n % tm == 0
    dp = p2d.shape[1]
    ff = wg.shape[2]
    sff = sg.shape[1]
    row = pl.BlockSpec((tm, d), lambda i, j: (i, 0))
    return pl.pallas_call(
        _moe_kernel,
        grid=(n // tm, N_EXPERTS // eb),
        in_specs=[row, pl.BlockSpec((tm, dp), lambda i, j: (i, 0)),
                  _full((N_EXPERTS, d)), _full((N_EXPERTS, 1)),
                  pl.BlockSpec((eb, d, ff), lambda i, j: (j, 0, 0)),
                  pl.BlockSpec((eb, d, ff), lambda i, j: (j, 0, 0)),
                  pl.BlockSpec((eb, ff, d), lambda i, j: (j, 0, 0)),
                  _full((d, sff)), _full((d, sff)), _full((sff, d)),
                  _full((1, d)), _full((1, d)), _full((d, d)), _full((dp, d))],
        out_specs=row,
        out_shape=jax.ShapeDtypeStruct((n, d), F32),
        scratch_shapes=[pltpu.VMEM((tm, d), BF16), pltpu.VMEM((N_EXPERTS, tm), F32),
                        pltpu.VMEM((tm, d), F32)],
        compiler_params=_params(("arbitrary", "arbitrary")),
        name="moe",
    )(x2d, p2d, wrt, rbias, wg, wu, wd, sg, su, sd, g2, b2, pg, pp)


def kernel(x_prompt, x_sample, p_prompt, p_sample, state_conv, state_hgrn, w_in, b_in, hgrn_lb, conv_w, conv_b, conv_ln_g, conv_ln_b, w_conv_out, hgrn_norm_g, w_hgrn_out, w_o, ln1_g, ln1_b, w_router, router_bias, w_exp_gate, w_exp_up, w_exp_down, w_sh_gate, w_sh_up, w_sh_down, ln2_g, ln2_b, w_ple_gate, w_ple_proj):
    assert w_in.shape[0] == DEPTH == 1
    bp, tp, d = x_prompt.shape
    bs, ts, _ = x_sample.shape
    nbuf = CONV_WIDTH - 1
    i = 0
    layer = lambda a: a.reshape(a.shape[1:])
    row = lambda a: a.reshape(1, -1)
    bf = lambda a: layer(a).astype(BF16)

    w_in_b, wco, who, wo = bf(w_in), bf(w_conv_out), bf(w_hgrn_out), bf(w_o)
    weg, weu, wed = bf(w_exp_gate), bf(w_exp_up), bf(w_exp_down)
    wsg, wsu, wsd = bf(w_sh_gate), bf(w_sh_up), bf(w_sh_down)
    wpg, wpp = bf(w_ple_gate), bf(w_ple_proj)
    wrt = layer(w_router).T
    rbias = router_bias.reshape(N_EXPERTS, 1)
    gnorm = row(hgrn_norm_g)

    def tail(x2d, p2d, ha, hb, ga, gb):
        x1 = _merge(x2d, ha, hb, ga, gb, wco, who, wo, row(ln1_g), row(ln1_b))
        return _moe(x1, p2d, wrt, rbias, weg, weu, wed, wsg, wsu, wsd, row(ln2_g), row(ln2_b), wpg, wpp)

    xp = x_prompt.reshape(bp * tp, d)
    u, q, k, v, lf, sg, ga, gb = _proj(xp, w_in_b, row(b_in), hgrn_lb, i)
    seq = lambda a: a.reshape(bp, tp, d)
    ha = _conv_prompt(seq(u), layer(conv_w), row(conv_b), row(conv_ln_g), row(conv_ln_b))
    hb, hgrn_p = _hgrn_prompt(seq(q), seq(k), seq(v), seq(lf), seq(sg), gnorm,
                              jnp.zeros((bp, HGRN_HEADS, HEAD_DIM, HEAD_DIM), F32))
    y_p = tail(xp, p_prompt.reshape(bp * tp, -1), ha.reshape(bp * tp, d), hb.reshape(bp * tp, d), ga, gb)
    conv_p = seq(u)[:, tp - nbuf:, :]

    tmaj = lambda a: jnp.swapaxes(a, 0, 1).reshape(ts * bs, -1)
    xs = tmaj(x_sample)
    u, q, k, v, lf, sg, ga, gb = _proj(xs, w_in_b, row(b_in), hgrn_lb, i)
    seq = lambda a: a.reshape(ts, bs, d)
    ha, conv_s = _conv_sample(seq(u), layer(state_conv), layer(conv_w), row(conv_b), row(conv_ln_g),
                              row(conv_ln_b))
    hb, hgrn_s = _hgrn_sample(seq(q), seq(k), seq(v), seq(lf), seq(sg), gnorm, layer(state_hgrn))
    y_s = tail(xs, tmaj(layer(p_sample)), ha.reshape(ts * bs, d), hb.reshape(ts * bs, d), ga, gb)
    y_s = jnp.swapaxes(y_s.reshape(ts, bs, d), 0, 1)

    return (y_p.reshape(bp, tp, d), y_s, conv_p[None], hgrn_p[None], conv_s[None], hgrn_s[None])
```

```python
import functools

import jax
import jax.numpy as jnp
from jax import lax
from jax.experimental import pallas as pl
from jax.experimental.pallas import tpu as pltpu

F32 = jnp.float32
BF16 = jnp.bfloat16

D_MODEL = 1024
CONV_WIDTH = 31
HGRN_HEADS = 8
HEAD_DIM = D_MODEL // HGRN_HEADS
N_EXPERTS = 64
N_GROUPS = 8
GROUP_SIZE = N_EXPERTS // N_GROUPS
TOPK_GROUPS = 4
TOP_K = 8
EXPERT_FF = 256
ROUTED_SCALE = 2.5
LN_EPS = 1e-5
LOG2_E = 1.4426950408889634
NEG_BIG = 1e30
SUBLANES = 8
LANES = 128
DEPTH = 1
DEEPNORM_ALPHA = (2.0 * DEPTH) ** 0.25

V7X_VMEM_BYTES = 64 * 1024 * 1024
VMEM_LIMIT = V7X_VMEM_BYTES - 8 * 1024 * 1024

TOKEN_BLOCK = 512
CONV_ROWS = 128
NORM_ROWS = 64
SAMPLE_SEQ_BLOCK = 32
CONV_HALO = 32
HGRN_BLOCK = 512
HGRN_HEAD_GROUP = 8
HGRN_CHUNK = 64
HGRN_SUB = 8
HGRN_SAMPLE_BLOCK = 8
SAMPLE_PAD = 16
MOE_TOKEN_BLOCK = 1024
EXPERTS_PER_STEP = 4


def _sigmoid(x):
    return 1.0 / (1.0 + jnp.exp(-x))


def _silu(x):
    return x * _sigmoid(x)


def _layer_norm(x, g, b):
    mu = jnp.mean(x, axis=-1, keepdims=True)
    xc = x - mu
    var = jnp.mean(xc * xc, axis=-1, keepdims=True)
    return xc * lax.rsqrt(var + LN_EPS) * g + b


def _dot(a, b):
    return jnp.dot(a, b, preferred_element_type=F32)


def _dot_nt(a, b):
    return lax.dot_general(a, b, (((1,), (1,)), ((), ())), preferred_element_type=F32)


def _full(shape):
    return pl.BlockSpec(shape, lambda *_: (0,) * len(shape))


def _params(sem):
    return pltpu.CompilerParams(dimension_semantics=sem, vmem_limit_bytes=VMEM_LIMIT)


def _proj_columns(x_ref, w_ref, b_ref, lbp_ref, layer, emit_u, q_ref, k_ref, v_ref, lf_ref, sg_ref, ga_ref, gb_ref):
    D = D_MODEL
    xb = x_ref[...].astype(BF16)

    def col(j):
        return _dot(xb, w_ref[:, j * D:(j + 1) * D]) + b_ref[:, j * D:(j + 1) * D]

    emit_u(col(0) * _sigmoid(col(1)))
    hl = lbp_ref[...]
    e = jnp.exp(hl - jnp.max(hl, axis=0, keepdims=True))
    lb = jnp.sum(e[:layer + 1], axis=0, keepdims=True) / jnp.sum(e, axis=0, keepdims=True)
    fz = col(2)
    lf_ref[...] = jnp.log(lb + (1.0 - lb) * _sigmoid(fz))
    k_ref[...] = ((1.0 - lb) * _sigmoid(-fz)).astype(k_ref.dtype)
    v_ref[...] = col(3).astype(v_ref.dtype)
    q_ref[...] = _silu(col(4)).astype(q_ref.dtype)
    sg_ref[...] = _silu(col(5)).astype(sg_ref.dtype)
    ga_ref[...] = _sigmoid(col(6)).astype(ga_ref.dtype)
    gb_ref[...] = _sigmoid(col(7)).astype(gb_ref.dtype)


def _proj_kernel(x_ref, w_ref, b_ref, lbp_ref, u_ref, q_ref, k_ref, v_ref, lf_ref, sg_ref,
                 ga_ref, gb_ref, *, layer):
    def emit_u(u):
        u_ref[...] = u

    _proj_columns(x_ref, w_ref, b_ref, lbp_ref, layer, emit_u, q_ref, k_ref, v_ref, lf_ref, sg_ref, ga_ref, gb_ref)


def _proj(x2d, w_in, b_in, hgrn_lb, layer):
    n, d = x2d.shape
    tm = TOKEN_BLOCK
    cols = w_in.shape[1]
    row = pl.BlockSpec((tm, d), lambda i: (i, 0))
    outs = [jax.ShapeDtypeStruct((n, d), dt) for dt in (F32, BF16, BF16, BF16, F32, BF16, BF16, BF16)]
    return pl.pallas_call(
        functools.partial(_proj_kernel, layer=layer),
        grid=(n // tm,),
        in_specs=[row,
                  pl.BlockSpec((d, cols), lambda i: (0, 0), pipeline_mode=pl.Buffered(1)),
                  _full((1, cols)), _full(hgrn_lb.shape)],
        out_specs=[row] * 8,
        out_shape=outs,
        compiler_params=_params(("arbitrary",)),
        name="proj",
    )(x2d, w_in, b_in, hgrn_lb)


def _conv_tap_groups():
    shift = CONV_HALO - (CONV_WIDTH - 1)
    groups = [[] for _ in range(SUBLANES)]
    for j in range(CONV_WIDTH):
        groups[(j + shift) % SUBLANES].append((j, (j + shift) // SUBLANES))
    return groups


def _conv_rows(ext_ref, cw_ref, cb_ref, y_ref, base, rows):
    for l in range(D_MODEL // LANES):
        lanes = slice(l * LANES, (l + 1) * LANES)
        acc = None
        for res, taps in enumerate(_conv_tap_groups()):
            part = None
            for j, a in taps:
                term = cw_ref[j:j + 1, lanes] * ext_ref[pl.ds(base + SUBLANES * a, rows + SUBLANES), lanes]
                part = term if part is None else part + term
            part = part[res:res + rows, :]
            acc = part if acc is None else acc + part
        y_ref[pl.ds(base, rows), lanes] = acc + cb_ref[:, lanes]


def _proj_conv_kernel(x_ref, w_ref, b_ref, lbp_ref, cw_ref, cb_ref, g_ref, bb_ref,
                      h_ref, tail_ref, q_ref, k_ref, v_ref, lf_ref, sg_ref, ga_ref, gb_ref,
                      ext_ref, y_ref, *, layer):
    tm = x_ref.shape[0]

    @pl.when(pl.program_id(1) == 0)
    def _():
        ext_ref[0:CONV_HALO, :] = jnp.zeros((CONV_HALO, D_MODEL), F32)
        ext_ref[CONV_HALO + tm:CONV_HALO + tm + SUBLANES, :] = jnp.zeros((SUBLANES, D_MODEL), F32)

    def emit_u(u):
        ext_ref[CONV_HALO:CONV_HALO + tm, :] = u
        tail_ref[...] = u[tm - CONV_HALO:tm, :]
        for r in range(tm // CONV_ROWS):
            _conv_rows(ext_ref, cw_ref, cb_ref, y_ref, r * CONV_ROWS, CONV_ROWS)
        for r in range(tm // NORM_ROWS):
            rows = slice(r * NORM_ROWS, (r + 1) * NORM_ROWS)
            h_ref[rows, :] = _silu(_layer_norm(y_ref[rows, :], g_ref[...], bb_ref[...])).astype(h_ref.dtype)
        ext_ref[0:CONV_HALO, :] = ext_ref[tm:tm + CONV_HALO, :]

    _proj_columns(x_ref, w_ref, b_ref, lbp_ref, layer, emit_u, q_ref, k_ref, v_ref, lf_ref, sg_ref, ga_ref, gb_ref)


def _proj_conv(x2d, seq_len, w_in, b_in, hgrn_lb, layer, conv_w, conv_b, g, b):
    n, d = x2d.shape
    tm = TOKEN_BLOCK
    nt = seq_len // tm
    cols = w_in.shape[1]
    row = pl.BlockSpec((tm, d), lambda s, j: (s * nt + j, 0))
    outs = [jax.ShapeDtypeStruct((n, d), BF16), jax.ShapeDtypeStruct((n // seq_len * CONV_HALO, d), F32)]
    outs += [jax.ShapeDtypeStruct((n, d), dt) for dt in (BF16, BF16, BF16, F32, BF16, BF16, BF16)]
    return pl.pallas_call(
        functools.partial(_proj_conv_kernel, layer=layer),
        grid=(n // seq_len, nt),
        in_specs=[row,
                  pl.BlockSpec((d, cols), lambda s, j: (0, 0), pipeline_mode=pl.Buffered(1)),
                  _full((1, cols)), _full(hgrn_lb.shape),
                  _full(conv_w.shape), _full((1, d)), _full((1, d)), _full((1, d))],
        out_specs=[row, pl.BlockSpec((CONV_HALO, d), lambda s, j: (s, 0))] + [row] * 7,
        out_shape=outs,
        scratch_shapes=[pltpu.VMEM((CONV_HALO + tm + SUBLANES, d), F32), pltpu.VMEM((tm, d), F32)],
        compiler_params=_params(("arbitrary", "arbitrary")),
        name="proj_conv",
    )(x2d, w_in, b_in, hgrn_lb, conv_w, conv_b, g, b)


def _conv_sample_kernel(u_ref, st_ref, cw_ref, cb_ref, g_ref, b_ref, h_ref, new_ref):
    t_new, sb, d = u_ref.shape
    nbuf = st_ref.shape[1] // d

    def ext_row(r, rows):
        if r < nbuf:
            return st_ref[rows, r * d:(r + 1) * d]
        return u_ref[r - nbuf, rows, :]

    def body(gi, carry):
        rows = pl.ds(pl.multiple_of(gi * SUBLANES, SUBLANES), SUBLANES)
        for t in range(t_new):
            acc = jnp.zeros((SUBLANES, d), F32) + cb_ref[...]
            for j in range(CONV_WIDTH):
                acc = acc + cw_ref[j:j + 1, :] * ext_row(t + j, rows)
            h_ref[t, rows, :] = _silu(_layer_norm(acc, g_ref[...], b_ref[...])).astype(h_ref.dtype)
        return carry

    lax.fori_loop(0, sb // SUBLANES, body, 0)
    new_ref[:, 0:(nbuf - t_new) * d] = st_ref[:, t_new * d:nbuf * d]
    for t in range(t_new):
        new_ref[:, (nbuf - t_new + t) * d:(nbuf - t_new + t + 1) * d] = u_ref[t]


def _conv_sample(u, state, conv_w, conv_b, g, b):
    t, bsz, d = u.shape
    nbuf = state.shape[1]
    sb = SAMPLE_SEQ_BLOCK
    st2 = state.reshape(bsz, nbuf * d)
    h, new = pl.pallas_call(
        _conv_sample_kernel,
        grid=(bsz // sb,),
        in_specs=[pl.BlockSpec((t, sb, d), lambda i: (0, i, 0)),
                  pl.BlockSpec((sb, nbuf * d), lambda i: (i, 0)),
                  _full(conv_w.shape), _full((1, d)), _full((1, d)), _full((1, d))],
        out_specs=[pl.BlockSpec((t, sb, d), lambda i: (0, i, 0)),
                   pl.BlockSpec((sb, nbuf * d), lambda i: (i, 0))],
        out_shape=[jax.ShapeDtypeStruct((t, bsz, d), F32),
                   jax.ShapeDtypeStruct((bsz, nbuf * d), F32)],
        compiler_params=_params(("arbitrary",)),
        name="conv_sample",
    )(u, st2, conv_w, conv_b, g, b)
    return h, new.reshape(bsz, nbuf, d)


def _hgrn_chunk(q, k, v, g, st, consts, chunk, sub):
    _, blk_mask, neg_masks = consts
    nb = chunk // sub
    lcum = g
    l_end = lcum[chunk - 1:chunk, :]

    o = _dot_nt((q * jnp.exp2(lcum)).astype(BF16), st.astype(BF16))

    if nb > 1:
        starts = [lcum[i * sub - 1:i * sub, :] for i in range(1, nb)]
        l_start = jnp.concatenate(
            [jnp.zeros((sub, HEAD_DIM), F32)]
            + [jnp.broadcast_to(s, (sub, HEAD_DIM)) for s in starts], axis=0)
        q_rel = (q * jnp.exp2(lcum - l_start)).astype(BF16)
        k_stack = jnp.concatenate(
            [k[0:i * sub] * jnp.exp2(starts[i - 1] - lcum[0:i * sub]) for i in range(1, nb)],
            axis=0).astype(BF16)
        v_stack = jnp.concatenate([v[0:i * sub] for i in range(1, nb)], axis=0).astype(BF16)
        scores = _dot_nt(q_rel, k_stack) * blk_mask
        o = o + _dot(scores.astype(BF16), v_stack)

    diag = []
    for i in range(nb):
        sl = slice(i * sub, (i + 1) * sub)
        qi, ki, li = q[sl], k[sl], lcum[sl]
        od = None
        for s in range(sub):
            dec = jnp.exp2(li - li[s:s + 1, :] + neg_masks[s])
            score = jnp.sum(qi * (ki[s:s + 1, :] * dec), axis=-1, keepdims=True)
            term = score * v[i * sub + s:i * sub + s + 1, :]
            od = term if od is None else od + term
        diag.append(od)
    o = o + (jnp.concatenate(diag, axis=0) if nb > 1 else diag[0])

    k_end = (k * jnp.exp2(l_end - lcum)).astype(BF16)
    st_new = st * jnp.exp2(l_end) + _dot(v.T.astype(BF16), k_end)
    return o, st_new


def _hgrn_consts(chunk, sub):
    nb = chunk // sub
    r = lax.broadcasted_iota(jnp.int32, (chunk, chunk), 0)
    c = lax.broadcasted_iota(jnp.int32, (chunk, chunk), 1)
    tri = jnp.where(c <= r, 1.0, 0.0).astype(BF16)
    row = lax.broadcasted_iota(jnp.int32, (sub, HEAD_DIM), 0)
    neg_masks = [jnp.where(row >= s, 0.0, -NEG_BIG).astype(F32) for s in range(sub)]
    blk_mask = None
    if nb > 1:
        width = sub * nb * (nb - 1) // 2
        rb = lax.broadcasted_iota(jnp.int32, (chunk, width), 0) // sub
        cc = lax.broadcasted_iota(jnp.int32, (chunk, width), 1)
        blk_mask = jnp.zeros((chunk, width), F32)
        off = 0
        for i in range(1, nb):
            hit = jnp.where(rb == i, jnp.where(cc >= off, jnp.where(cc < off + i * sub, 1.0, 0.0), 0.0), 0.0)
            blk_mask = blk_mask + hit
            off += i * sub
    return tri, blk_mask, neg_masks


def _cumsum_rows(tri_bf, g):
    g1 = g.astype(BF16)
    r1 = g - g1.astype(F32)
    g2 = r1.astype(BF16)
    g3 = (r1 - g2.astype(F32)).astype(BF16)
    return _dot(tri_bf, g1) + _dot(tri_bf, g2) + _dot(tri_bf, g3)


def _head_norm(o, gn):
    return o * lax.rsqrt(jnp.mean(o * o, axis=-1, keepdims=True) + LN_EPS) * gn


def _hgrn_prompt_kernel(q_ref, k_ref, v_ref, lf_ref, sg_ref, gn_ref, s0_ref, o_ref, s_ref, st_ref, *,
                        chunk, sub):
    tb = q_ref.shape[1]
    heads = q_ref.shape[2] // HEAD_DIM
    j = pl.program_id(2)
    consts = _hgrn_consts(chunk, sub)
    gn = gn_ref[...]

    @pl.when(j == 0)
    def _():
        for h in range(heads):
            st_ref[h] = s0_ref[0, h].T

    def body(n, carry):
        rows = pl.ds(pl.multiple_of(n * chunk, chunk), chunk)
        lcum = _cumsum_rows(consts[0], lf_ref[0, rows, :]) * LOG2_E
        for h in range(heads):
            lanes = slice(h * HEAD_DIM, (h + 1) * HEAD_DIM)
            o, st = _hgrn_chunk(q_ref[0, rows, lanes].astype(F32), k_ref[0, rows, lanes].astype(F32),
                                v_ref[0, rows, lanes].astype(F32), lcum[:, lanes], st_ref[h],
                                consts, chunk, sub)
            st_ref[h] = st
            o_ref[0, rows, lanes] = (_head_norm(o, gn) * sg_ref[0, rows, lanes].astype(F32)).astype(o_ref.dtype)
        return carry

    lax.fori_loop(0, tb // chunk, body, 0)

    @pl.when(j == pl.num_programs(2) - 1)
    def _():
        for h in range(heads):
            s_ref[0, h] = st_ref[h].T


def _hgrn_prompt(q, k, v, lf, sg, gnorm, s0):
    bsz, t, d = q.shape
    tb = HGRN_BLOCK
    hg = HGRN_HEAD_GROUP
    blk = pl.BlockSpec((1, tb, hg * HEAD_DIM), lambda b, g, j: (b, j, g))
    sblk = pl.BlockSpec((1, hg, HEAD_DIM, HEAD_DIM), lambda b, g, j: (b, g, 0, 0))
    return pl.pallas_call(
        functools.partial(_hgrn_prompt_kernel, chunk=HGRN_CHUNK, sub=HGRN_SUB),
        grid=(bsz, HGRN_HEADS // hg, t // tb),
        in_specs=[blk, blk, blk, blk, blk, _full((1, HEAD_DIM)), sblk],
        out_specs=[blk, sblk],
        out_shape=[jax.ShapeDtypeStruct((bsz, t, d), BF16),
                   jax.ShapeDtypeStruct(s0.shape, F32)],
        scratch_shapes=[pltpu.VMEM((hg, HEAD_DIM, HEAD_DIM), F32)],
        compiler_params=_params(("arbitrary", "arbitrary", "arbitrary")),
        name="hgrn_prompt",
    )(q, k, v, lf, sg, gnorm, s0)


def _hgrn_sample_kernel(q_ref, k_ref, v_ref, lf_ref, sg_ref, gn_ref, s0_ref, o_ref, s_ref):
    sb, t_new, d_model = q_ref.shape
    gn = gn_ref[...]
    pad_rows = SAMPLE_PAD - t_new - 1

    def body(b, carry):
        q, k, v, sg = q_ref[b], k_ref[b], v_ref[b], sg_ref[b]
        lf = lf_ref[b]
        lrow = []
        for t in range(t_new):
            lrow.append(lf[t:t + 1, :] if t == 0 else lrow[-1] + lf[t:t + 1, :])
        lcum = jnp.concatenate(lrow, axis=0)
        l_end = lrow[-1]
        qe = q * jnp.exp(lcum)
        ke = k * jnp.exp(l_end - lcum)
        f_end = jnp.exp(l_end)
        zeros = jnp.zeros((pad_rows, d_model), F32)
        zero_row = jnp.zeros((1, d_model), F32)
        qe_p = jnp.concatenate([qe, zero_row, zeros], axis=0)
        ke_p = jnp.concatenate([ke, f_end, zeros], axis=0)
        v_p = jnp.concatenate([v, zero_row, zeros], axis=0)
        for h in range(HGRN_HEADS):
            lanes = slice(h * HEAD_DIM, (h + 1) * HEAD_DIM)
            s0 = s0_ref[b, h]
            o = _dot(qe_p[:, lanes].astype(BF16), s0.astype(BF16))[0:t_new]
            rows = []
            for t in range(t_new):
                ot = None
                for s in range(t + 1):
                    prod = q[t:t + 1, lanes] * k[s:s + 1, lanes]
                    if s < t:
                        prod = prod * jnp.exp(lrow[t][:, lanes] - lrow[s][:, lanes])
                    term = jnp.sum(prod, axis=-1, keepdims=True) * v[s:s + 1, lanes]
                    ot = term if ot is None else ot + term
                rows.append(ot)
            o = o + jnp.concatenate(rows, axis=0)
            ke_t = ke_p[:, lanes].T
            decay = ke_t[:, t_new:t_new + 1]
            s_ref[b, h] = s0 * decay + _dot(ke_t.astype(BF16), v_p[:, lanes].astype(BF16))
            o_ref[b, :, lanes] = _head_norm(o, gn) * sg[:, lanes]
        return carry

    lax.fori_loop(0, sb, body, 0)


def _hgrn_sample(q, k, v, lf, sg, gnorm, s0):
    bsz, t, d = q.shape
    sb = HGRN_SAMPLE_BLOCK
    blk = pl.BlockSpec((sb, t, d), lambda g: (g, 0, 0))
    sblk = pl.BlockSpec((sb, HGRN_HEADS, HEAD_DIM, HEAD_DIM), lambda g: (g, 0, 0, 0))
    return pl.pallas_call(
        _hgrn_sample_kernel,
        grid=(bsz // sb,),
        in_specs=[blk, blk, blk, blk, blk, _full((1, HEAD_DIM)), sblk],
        out_specs=[blk, sblk],
        out_shape=[jax.ShapeDtypeStruct((bsz, t, d), F32),
                   jax.ShapeDtypeStruct(s0.shape, F32)],
        compiler_params=_params(("arbitrary",)),
        name="hgrn_sample",
    )(q, k, v, lf, sg, gnorm, s0)


def _merge_kernel(x_ref, ha_ref, hb_ref, ga_ref, gb_ref, wa_ref, wb_ref, wo_ref, g_ref, b_ref, o_ref):
    ya = _dot(ha_ref[...].astype(BF16), wa_ref[...])
    yb = _dot(hb_ref[...].astype(BF16), wb_ref[...])
    mixed = ga_ref[...].astype(F32) * ya + gb_ref[...].astype(F32) * yb
    z = DEEPNORM_ALPHA * x_ref[...] + _dot(mixed.astype(BF16), wo_ref[...])
    o_ref[...] = _layer_norm(z, g_ref[...], b_ref[...])


def _merge(x2d, ha, hb, ga, gb, wa, wb, wo, g, b):
    n, d = x2d.shape
    tm = TOKEN_BLOCK
    row = pl.BlockSpec((tm, d), lambda i: (i, 0))
    return pl.pallas_call(
        _merge_kernel,
        grid=(n // tm,),
        in_specs=[row] * 5 + [_full((d, d))] * 3 + [_full((1, d))] * 2,
        out_specs=row,
        out_shape=jax.ShapeDtypeStruct((n, d), F32),
        compiler_params=_params(("arbitrary",)),
        name="merge",
    )(x2d, ha, hb, ga, gb, wa, wb, wo, g, b)


def _route_t(s, bias):
    n = s.shape[1]
    neg = -jnp.inf
    sb = (s + bias).reshape(N_GROUPS, GROUP_SIZE, n)
    s3 = s.reshape(N_GROUPS, GROUP_SIZE, n)
    e_in_g = lax.broadcasted_iota(jnp.int32, sb.shape, 1)
    m1 = jnp.max(sb, axis=1, keepdims=True)
    first = jnp.min(jnp.where(sb == m1, e_in_g, GROUP_SIZE), axis=1, keepdims=True)
    m2 = jnp.max(jnp.where(e_in_g == first, neg, sb), axis=1, keepdims=True)
    gscore = (m1 + m2)[:, 0, :]
    gid = lax.broadcasted_iota(jnp.int32, gscore.shape, 0)
    gsel = jnp.zeros(gscore.shape, F32)
    for _ in range(TOPK_GROUPS):
        gm = jnp.max(gscore, axis=0, keepdims=True)
        pick = jnp.min(jnp.where(gscore == gm, gid, N_GROUPS), axis=0, keepdims=True)
        hit = gid == pick
        gsel = jnp.where(hit, 1.0, gsel)
        gscore = jnp.where(hit, neg, gscore)
    cand = jnp.where(gsel[:, None, :] > 0.5, sb, neg)
    eid = lax.broadcasted_iota(jnp.int32, sb.shape, 0) * GROUP_SIZE + e_in_g
    esel = jnp.zeros(sb.shape, F32)
    for _ in range(TOP_K):
        em = jnp.max(jnp.max(cand, axis=1, keepdims=True), axis=0, keepdims=True)
        masked = jnp.where(cand == em, eid, N_EXPERTS)
        pick = jnp.min(jnp.min(masked, axis=1, keepdims=True), axis=0, keepdims=True)
        hit = eid == pick
        esel = jnp.where(hit, 1.0, esel)
        cand = jnp.where(hit, neg, cand)
    w = esel * s3
    tot = jnp.sum(jnp.sum(w, axis=1, keepdims=True), axis=0, keepdims=True)
    return (w / tot * ROUTED_SCALE).reshape(N_EXPERTS, n)


def _moe_kernel(x_ref, p_ref, wrt_ref, rb_ref, wg_ref, wu_ref, wd_ref, sg_ref, su_ref, sd_ref,
                g2_ref, b2_ref, pg_ref, pp_ref, o_ref, xb_ref, gt_ref, acc_ref):
    j = pl.program_id(1)
    eb = wg_ref.shape[0]

    @pl.when(j == 0)
    def _():
        x = x_ref[...]
        logits = lax.dot_general(wrt_ref[...], x, (((1,), (1,)), ((), ())),
                                 precision=lax.Precision.HIGHEST, preferred_element_type=F32)
        gt_ref[...] = _route_t(_sigmoid(logits), rb_ref[...])
        xb = x.astype(BF16)
        xb_ref[...] = xb
        hs = _silu(_dot(xb, sg_ref[...])) * _dot(xb, su_ref[...])
        acc_ref[...] = _dot(hs.astype(BF16), sd_ref[...])

    xb = xb_ref[...]
    first = j * eb
    slab = lax.shift_right_logical(first, 3) * SUBLANES
    gates = gt_ref[pl.ds(pl.multiple_of(slab, SUBLANES), SUBLANES), :].T
    in_slab = first - slab
    for r in range(eb):
        gate = gates[:, r:r + 1]
        for off in range(eb, SUBLANES, eb):
            gate = jnp.where(in_slab == off, gates[:, off + r:off + r + 1], gate)
        h = _silu(_dot(xb, wg_ref[r])) * _dot(xb, wu_ref[r])
        acc_ref[...] += _dot((h * gate).astype(BF16), wd_ref[r])

    @pl.when(j == pl.num_programs(1) - 1)
    def _():
        x2 = _layer_norm(DEEPNORM_ALPHA * x_ref[...] + acc_ref[...], g2_ref[...], b2_ref[...])
        gate = _sigmoid(_dot(x2.astype(BF16), pg_ref[...]))
        o_ref[...] = x2 + gate * _dot(p_ref[...].astype(BF16), pp_ref[...])


def _moe(x2d, p2d, wrt, rbias, wg, wu, wd, sg, su, sd, g2, b2, pg, pp):
    n, d = x2d.shape
    tm = MOE_TOKEN_BLOCK if n % MOE_TOKEN_BLOCK == 0 else TOKEN_BLOCK
    eb = EXPERTS_PER_STEP
    assert SUBLANES % eb == 0 and n % tm == 0
    dp = p2d.shape[1]
    ff = wg.shape[2]
    sff = sg.shape[1]
    row = pl.BlockSpec((tm, d), lambda i, j: (i, 0))
    return pl.pallas_call(
        _moe_kernel,
        grid=(n // tm, N_EXPERTS // eb),
        in_specs=[row, pl.BlockSpec((tm, dp), lambda i, j: (i, 0)),
                  _full((N_EXPERTS, d)), _full((N_EXPERTS, 1)),
                  pl.BlockSpec((eb, d, ff), lambda i, j: (j, 0, 0)),
                  pl.BlockSpec((eb, d, ff), lambda i, j: (j, 0, 0)),
                  pl.BlockSpec((eb, ff, d), lambda i, j: (j, 0, 0)),
                  _full((d, sff)), _full((d, sff)), _full((sff, d)),
                  _full((1, d)), _full((1, d)), _full((d, d)), _full((dp, d))],
        out_specs=row,
        out_shape=jax.ShapeDtypeStruct((n, d), F32),
        scratch_shapes=[pltpu.VMEM((tm, d), BF16), pltpu.VMEM((N_EXPERTS, tm), F32),
                        pltpu.VMEM((tm, d), F32)],
        compiler_params=_params(("arbitrary", "arbitrary")),
        name="moe",
    )(x2d, p2d, wrt, rbias, wg, wu, wd, sg, su, sd, g2, b2, pg, pp)


def kernel(x_prompt, x_sample, p_prompt, p_sample, state_conv, state_hgrn, w_in, b_in, hgrn_lb, conv_w, conv_b, conv_ln_g, conv_ln_b, w_conv_out, hgrn_norm_g, w_hgrn_out, w_o, ln1_g, ln1_b, w_router, router_bias, w_exp_gate, w_exp_up, w_exp_down, w_sh_gate, w_sh_up, w_sh_down, ln2_g, ln2_b, w_ple_gate, w_ple_proj):
    assert w_in.shape[0] == DEPTH == 1
    bp, tp, d = x_prompt.shape
    bs, ts, _ = x_sample.shape
    nbuf = CONV_WIDTH - 1
    i = 0
    layer = lambda a: a.reshape(a.shape[1:])
    row = lambda a: a.reshape(1, -1)
    bf = lambda a: layer(a).astype(BF16)

    w_in_b, wco, who, wo = bf(w_in), bf(w_conv_out), bf(w_hgrn_out), bf(w_o)
    weg, weu, wed = bf(w_exp_gate), bf(w_exp_up), bf(w_exp_down)
    wsg, wsu, wsd = bf(w_sh_gate), bf(w_sh_up), bf(w_sh_down)
    wpg, wpp = bf(w_ple_gate), bf(w_ple_proj)
    wrt = layer(w_router).T
    rbias = router_bias.reshape(N_EXPERTS, 1)
    gnorm = row(hgrn_norm_g)

    def tail(x2d, p2d, ha, hb, ga, gb):
        x1 = _merge(x2d, ha, hb, ga, gb, wco, who, wo, row(ln1_g), row(ln1_b))
        return _moe(x1, p2d, wrt, rbias, weg, weu, wed, wsg, wsu, wsd, row(ln2_g), row(ln2_b), wpg, wpp)

    xp = x_prompt.reshape(bp * tp, d)
    ha, u_tail, q, k, v, lf, sg, ga, gb = _proj_conv(xp, tp, w_in_b, row(b_in), hgrn_lb, i, layer(conv_w),
                                                     row(conv_b), row(conv_ln_g), row(conv_ln_b))
    seq = lambda a: a.reshape(bp, tp, d)
    hb, hgrn_p = _hgrn_prompt(seq(q), seq(k), seq(v), seq(lf), seq(sg), gnorm,
                              jnp.zeros((bp, HGRN_HEADS, HEAD_DIM, HEAD_DIM), F32))
    y_p = tail(xp, p_prompt.reshape(bp * tp, -1), ha, hb.reshape(bp * tp, d), ga, gb)
    conv_p = u_tail.reshape(bp, CONV_HALO, d)[:, CONV_HALO - nbuf:, :]

    tmaj = lambda a: jnp.swapaxes(a, 0, 1).reshape(ts * bs, -1)
    xs = tmaj(x_sample)
    u, q, k, v, lf, sg, ga, gb = _proj(xs, w_in_b, row(b_in), hgrn_lb, i)
    seq = lambda a: a.reshape(ts, bs, d)
    ha, conv_s = _conv_sample(seq(u), layer(state_conv), layer(conv_w), row(conv_b), row(conv_ln_g),
                              row(conv_ln_b))
    bmaj = lambda a: jnp.swapaxes(seq(a), 0, 1).astype(F32)
    hb, hgrn_s = _hgrn_sample(bmaj(q), bmaj(k), bmaj(v), bmaj(lf), bmaj(sg), gnorm, layer(state_hgrn))
    y_s = tail(xs, tmaj(layer(p_sample)), ha.reshape(ts * bs, d), tmaj(hb), ga, gb)
    y_s = jnp.swapaxes(y_s.reshape(ts, bs, d), 0, 1)

    return (y_p.reshape(bp, tp, d), y_s, conv_p[None], hgrn_p[None], conv_s[None], hgrn_s[None])
```

```python
import functools

import jax
import jax.numpy as jnp
from jax import lax
from jax.experimental import pallas as pl
from jax.experimental.pallas import tpu as pltpu

F32 = jnp.float32
BF16 = jnp.bfloat16

D_MODEL = 1024
CONV_WIDTH = 31
HGRN_HEADS = 8
HEAD_DIM = D_MODEL // HGRN_HEADS
N_EXPERTS = 64
N_GROUPS = 8
GROUP_SIZE = N_EXPERTS // N_GROUPS
TOPK_GROUPS = 4
TOP_K = 8
EXPERT_FF = 256
ROUTED_SCALE = 2.5
LN_EPS = 1e-5
LOG2_E = 1.4426950408889634
NEG_BIG = 1e30
SUBLANES = 8
LANES = 128
DEPTH = 1
DEEPNORM_ALPHA = (2.0 * DEPTH) ** 0.25

V7X_VMEM_BYTES = 64 * 1024 * 1024
VMEM_LIMIT = V7X_VMEM_BYTES - 8 * 1024 * 1024

TOKEN_BLOCK = 512
CONV_ROWS = 128
NORM_ROWS = 64
SAMPLE_SEQ_BLOCK = 32
CONV_HALO = 32
HGRN_BLOCK = 512
HGRN_HEAD_GROUP = 8
HGRN_CHUNK = 64
HGRN_MILD_LOG2 = 96.0
HGRN_SUB = 8
HGRN_SAMPLE_BLOCK = 8
SAMPLE_PAD = 16
MOE_TOKEN_BLOCK = 1024
EXPERTS_PER_STEP = 4


def _sigmoid(x):
    return 1.0 / (1.0 + jnp.exp(-x))


def _silu(x):
    return x * _sigmoid(x)


def _layer_norm(x, g, b):
    mu = jnp.mean(x, axis=-1, keepdims=True)
    xc = x - mu
    var = jnp.mean(xc * xc, axis=-1, keepdims=True)
    return xc * lax.rsqrt(var + LN_EPS) * g + b


def _dot(a, b):
    return jnp.dot(a, b, preferred_element_type=F32)


def _dot_nt(a, b):
    return lax.dot_general(a, b, (((1,), (1,)), ((), ())), preferred_element_type=F32)


def _full(shape):
    return pl.BlockSpec(shape, lambda *_: (0,) * len(shape))


def _params(sem):
    return pltpu.CompilerParams(dimension_semantics=sem, vmem_limit_bytes=VMEM_LIMIT)


def _proj_columns(x_ref, w_ref, b_ref, lbp_ref, layer, emit_u, q_ref, k_ref, v_ref, lf_ref, sg_ref, ga_ref, gb_ref):
    D = D_MODEL
    xb = x_ref[...].astype(BF16)

    def col(j):
        return _dot(xb, w_ref[:, j * D:(j + 1) * D]) + b_ref[:, j * D:(j + 1) * D]

    emit_u(col(0) * _sigmoid(col(1)))
    hl = lbp_ref[...]
    e = jnp.exp(hl - jnp.max(hl, axis=0, keepdims=True))
    lb = jnp.sum(e[:layer + 1], axis=0, keepdims=True) / jnp.sum(e, axis=0, keepdims=True)
    fz = col(2)
    lf_ref[...] = jnp.log(lb + (1.0 - lb) * _sigmoid(fz))
    k_ref[...] = ((1.0 - lb) * _sigmoid(-fz)).astype(k_ref.dtype)
    v_ref[...] = col(3).astype(v_ref.dtype)
    q_ref[...] = _silu(col(4)).astype(q_ref.dtype)
    sg_ref[...] = _silu(col(5)).astype(sg_ref.dtype)
    ga_ref[...] = _sigmoid(col(6)).astype(ga_ref.dtype)
    gb_ref[...] = _sigmoid(col(7)).astype(gb_ref.dtype)


def _proj_kernel(x_ref, w_ref, b_ref, lbp_ref, u_ref, q_ref, k_ref, v_ref, lf_ref, sg_ref,
                 ga_ref, gb_ref, *, layer):
    def emit_u(u):
        u_ref[...] = u

    _proj_columns(x_ref, w_ref, b_ref, lbp_ref, layer, emit_u, q_ref, k_ref, v_ref, lf_ref, sg_ref, ga_ref, gb_ref)


def _proj(x2d, w_in, b_in, hgrn_lb, layer):
    n, d = x2d.shape
    tm = TOKEN_BLOCK
    cols = w_in.shape[1]
    row = pl.BlockSpec((tm, d), lambda i: (i, 0))
    outs = [jax.ShapeDtypeStruct((n, d), dt) for dt in (F32, BF16, BF16, BF16, F32, BF16, BF16, BF16)]
    return pl.pallas_call(
        functools.partial(_proj_kernel, layer=layer),
        grid=(n // tm,),
        in_specs=[row,
                  pl.BlockSpec((d, cols), lambda i: (0, 0), pipeline_mode=pl.Buffered(1)),
                  _full((1, cols)), _full(hgrn_lb.shape)],
        out_specs=[row] * 8,
        out_shape=outs,
        compiler_params=_params(("arbitrary",)),
        name="proj",
    )(x2d, w_in, b_in, hgrn_lb)


def _conv_tap_groups():
    shift = CONV_HALO - (CONV_WIDTH - 1)
    groups = [[] for _ in range(SUBLANES)]
    for j in range(CONV_WIDTH):
        groups[(j + shift) % SUBLANES].append((j, (j + shift) // SUBLANES))
    return groups


def _conv_rows(ext_ref, cw_ref, cb_ref, y_ref, base, rows):
    for l in range(D_MODEL // LANES):
        lanes = slice(l * LANES, (l + 1) * LANES)
        acc = None
        for res, taps in enumerate(_conv_tap_groups()):
            part = None
            for j, a in taps:
                term = cw_ref[j:j + 1, lanes] * ext_ref[pl.ds(base + SUBLANES * a, rows + SUBLANES), lanes]
                part = term if part is None else part + term
            part = part[res:res + rows, :]
            acc = part if acc is None else acc + part
        y_ref[pl.ds(base, rows), lanes] = acc + cb_ref[:, lanes]


def _proj_conv_kernel(x_ref, w_ref, b_ref, lbp_ref, cw_ref, cb_ref, g_ref, bb_ref,
                      h_ref, tail_ref, q_ref, k_ref, v_ref, lf_ref, sg_ref, ga_ref, gb_ref,
                      ext_ref, y_ref, *, layer):
    tm = x_ref.shape[0]

    @pl.when(pl.program_id(1) == 0)
    def _():
        ext_ref[0:CONV_HALO, :] = jnp.zeros((CONV_HALO, D_MODEL), F32)
        ext_ref[CONV_HALO + tm:CONV_HALO + tm + SUBLANES, :] = jnp.zeros((SUBLANES, D_MODEL), F32)

    def emit_u(u):
        ext_ref[CONV_HALO:CONV_HALO + tm, :] = u
        tail_ref[...] = u[tm - CONV_HALO:tm, :]
        for r in range(tm // CONV_ROWS):
            _conv_rows(ext_ref, cw_ref, cb_ref, y_ref, r * CONV_ROWS, CONV_ROWS)
        for r in range(tm // NORM_ROWS):
            rows = slice(r * NORM_ROWS, (r + 1) * NORM_ROWS)
            h_ref[rows, :] = _silu(_layer_norm(y_ref[rows, :], g_ref[...], bb_ref[...])).astype(h_ref.dtype)
        ext_ref[0:CONV_HALO, :] = ext_ref[tm:tm + CONV_HALO, :]

    _proj_columns(x_ref, w_ref, b_ref, lbp_ref, layer, emit_u, q_ref, k_ref, v_ref, lf_ref, sg_ref, ga_ref, gb_ref)


def _proj_conv(x2d, seq_len, w_in, b_in, hgrn_lb, layer, conv_w, conv_b, g, b):
    n, d = x2d.shape
    tm = TOKEN_BLOCK
    nt = seq_len // tm
    cols = w_in.shape[1]
    row = pl.BlockSpec((tm, d), lambda s, j: (s * nt + j, 0))
    outs = [jax.ShapeDtypeStruct((n, d), BF16), jax.ShapeDtypeStruct((n // seq_len * CONV_HALO, d), F32)]
    outs += [jax.ShapeDtypeStruct((n, d), dt) for dt in (BF16, BF16, BF16, F32, BF16, BF16, BF16)]
    return pl.pallas_call(
        functools.partial(_proj_conv_kernel, layer=layer),
        grid=(n // seq_len, nt),
        in_specs=[row,
                  pl.BlockSpec((d, cols), lambda s, j: (0, 0), pipeline_mode=pl.Buffered(1)),
                  _full((1, cols)), _full(hgrn_lb.shape),
                  _full(conv_w.shape), _full((1, d)), _full((1, d)), _full((1, d))],
        out_specs=[row, pl.BlockSpec((CONV_HALO, d), lambda s, j: (s, 0))] + [row] * 7,
        out_shape=outs,
        scratch_shapes=[pltpu.VMEM((CONV_HALO + tm + SUBLANES, d), F32), pltpu.VMEM((tm, d), F32)],
        compiler_params=_params(("arbitrary", "arbitrary")),
        name="proj_conv",
    )(x2d, w_in, b_in, hgrn_lb, conv_w, conv_b, g, b)


def _conv_sample_kernel(u_ref, st_ref, cw_ref, cb_ref, g_ref, b_ref, h_ref, new_ref):
    t_new, sb, d = u_ref.shape
    nbuf = st_ref.shape[1] // d

    def ext_row(r, rows):
        if r < nbuf:
            return st_ref[rows, r * d:(r + 1) * d]
        return u_ref[r - nbuf, rows, :]

    def body(gi, carry):
        rows = pl.ds(pl.multiple_of(gi * SUBLANES, SUBLANES), SUBLANES)
        for t in range(t_new):
            acc = jnp.zeros((SUBLANES, d), F32) + cb_ref[...]
            for j in range(CONV_WIDTH):
                acc = acc + cw_ref[j:j + 1, :] * ext_row(t + j, rows)
            h_ref[t, rows, :] = _silu(_layer_norm(acc, g_ref[...], b_ref[...])).astype(h_ref.dtype)
        return carry

    lax.fori_loop(0, sb // SUBLANES, body, 0)
    new_ref[:, 0:(nbuf - t_new) * d] = st_ref[:, t_new * d:nbuf * d]
    for t in range(t_new):
        new_ref[:, (nbuf - t_new + t) * d:(nbuf - t_new + t + 1) * d] = u_ref[t]


def _conv_sample(u, state, conv_w, conv_b, g, b):
    t, bsz, d = u.shape
    nbuf = state.shape[1]
    sb = SAMPLE_SEQ_BLOCK
    st2 = state.reshape(bsz, nbuf * d)
    h, new = pl.pallas_call(
        _conv_sample_kernel,
        grid=(bsz // sb,),
        in_specs=[pl.BlockSpec((t, sb, d), lambda i: (0, i, 0)),
                  pl.BlockSpec((sb, nbuf * d), lambda i: (i, 0)),
                  _full(conv_w.shape), _full((1, d)), _full((1, d)), _full((1, d))],
        out_specs=[pl.BlockSpec((t, sb, d), lambda i: (0, i, 0)),
                   pl.BlockSpec((sb, nbuf * d), lambda i: (i, 0))],
        out_shape=[jax.ShapeDtypeStruct((t, bsz, d), F32),
                   jax.ShapeDtypeStruct((bsz, nbuf * d), F32)],
        compiler_params=_params(("arbitrary",)),
        name="conv_sample",
    )(u, st2, conv_w, conv_b, g, b)
    return h, new.reshape(bsz, nbuf, d)


def _hgrn_chunk(q, k, v, g, st, consts, chunk, sub):
    _, blk_mask, neg_masks = consts
    nb = chunk // sub
    lcum = g
    l_end = lcum[chunk - 1:chunk, :]

    o = _dot_nt((q * jnp.exp2(lcum)).astype(BF16), st.astype(BF16))

    if nb > 1:
        starts = [lcum[i * sub - 1:i * sub, :] for i in range(1, nb)]
        l_start = jnp.concatenate(
            [jnp.zeros((sub, HEAD_DIM), F32)]
            + [jnp.broadcast_to(s, (sub, HEAD_DIM)) for s in starts], axis=0)
        q_rel = (q * jnp.exp2(lcum - l_start)).astype(BF16)
        k_stack = jnp.concatenate(
            [k[0:i * sub] * jnp.exp2(starts[i - 1] - lcum[0:i * sub]) for i in range(1, nb)],
            axis=0).astype(BF16)
        v_stack = jnp.concatenate([v[0:i * sub] for i in range(1, nb)], axis=0).astype(BF16)
        scores = _dot_nt(q_rel, k_stack) * blk_mask
        o = o + _dot(scores.astype(BF16), v_stack)

    diag = []
    for i in range(nb):
        sl = slice(i * sub, (i + 1) * sub)
        qi, ki, li = q[sl], k[sl], lcum[sl]
        od = None
        for s in range(sub):
            dec = jnp.exp2(li - li[s:s + 1, :] + neg_masks[s])
            score = jnp.sum(qi * (ki[s:s + 1, :] * dec), axis=-1, keepdims=True)
            term = score * v[i * sub + s:i * sub + s + 1, :]
            od = term if od is None else od + term
        diag.append(od)
    o = o + (jnp.concatenate(diag, axis=0) if nb > 1 else diag[0])

    k_end = (k * jnp.exp2(l_end - lcum)).astype(BF16)
    st_new = st * jnp.exp2(l_end) + _dot(v.T.astype(BF16), k_end)
    return o, st_new


def _hgrn_chunk_mild(q, k, v, g, st, causal):
    chunk = q.shape[0]
    l_end = g[chunk - 1:chunk, :]
    q_dec = (q * jnp.exp2(g)).astype(BF16)
    k_inv = (k * jnp.exp2(-g)).astype(BF16)
    scores = jnp.where(causal, _dot_nt(q_dec, k_inv), 0.0).astype(BF16)
    o = _dot(jnp.concatenate([q_dec, scores], axis=1),
             jnp.concatenate([st.T.astype(BF16), v.astype(BF16)], axis=0))
    st_new = (st + _dot(v.T.astype(BF16), k_inv)) * jnp.exp2(l_end)
    return o, st_new


def _hgrn_consts(chunk, sub):
    nb = chunk // sub
    r = lax.broadcasted_iota(jnp.int32, (chunk, chunk), 0)
    c = lax.broadcasted_iota(jnp.int32, (chunk, chunk), 1)
    tri = jnp.where(c <= r, 1.0, 0.0).astype(BF16)
    row = lax.broadcasted_iota(jnp.int32, (sub, HEAD_DIM), 0)
    neg_masks = [jnp.where(row >= s, 0.0, -NEG_BIG).astype(F32) for s in range(sub)]
    blk_mask = None
    if nb > 1:
        width = sub * nb * (nb - 1) // 2
        rb = lax.broadcasted_iota(jnp.int32, (chunk, width), 0) // sub
        cc = lax.broadcasted_iota(jnp.int32, (chunk, width), 1)
        blk_mask = jnp.zeros((chunk, width), F32)
        off = 0
        for i in range(1, nb):
            hit = jnp.where(rb == i, jnp.where(cc >= off, jnp.where(cc < off + i * sub, 1.0, 0.0), 0.0), 0.0)
            blk_mask = blk_mask + hit
            off += i * sub
    return tri, blk_mask, neg_masks


def _cumsum_rows(tri_bf, g):
    g1 = g.astype(BF16)
    r1 = g - g1.astype(F32)
    g2 = r1.astype(BF16)
    g3 = (r1 - g2.astype(F32)).astype(BF16)
    return _dot(tri_bf, g1) + _dot(tri_bf, g2) + _dot(tri_bf, g3)


def _head_norm(o, gn):
    return o * lax.rsqrt(jnp.mean(o * o, axis=-1, keepdims=True) + LN_EPS) * gn


def _hgrn_prompt_kernel(q_ref, k_ref, v_ref, lf_ref, sg_ref, gn_ref, s0_ref, o_ref, s_ref,
                        st_ref, lc_ref, mild_ref, *, chunk, sub):
    tb = q_ref.shape[1]
    n_chunks = tb // chunk
    heads = q_ref.shape[2] // HEAD_DIM
    j = pl.program_id(2)
    consts = _hgrn_consts(chunk, sub)
    gn = gn_ref[...]

    @pl.when(j == 0)
    def _():
        for h in range(heads):
            st_ref[h] = s0_ref[0, h].T

    causal = (lax.broadcasted_iota(jnp.int32, (chunk, chunk), 1)
              <= lax.broadcasted_iota(jnp.int32, (chunk, chunk), 0))

    def prepare(n, slot):
        rows = pl.ds(pl.multiple_of(n * chunk, chunk), chunk)
        lcum = _cumsum_rows(consts[0], lf_ref[0, rows, :]) * LOG2_E
        lc_ref[slot] = lcum
        mild_ref[slot] = (jnp.min(lcum[chunk - 1:chunk, :]) >= -HGRN_MILD_LOG2).astype(jnp.int32)

    prepare(0, 0)

    def body(n, carry):
        rows = pl.ds(pl.multiple_of(n * chunk, chunk), chunk)
        slot = lax.rem(n, 2)
        mild = mild_ref[slot] == 1

        def run(step):
            for h in range(heads):
                lanes = slice(h * HEAD_DIM, (h + 1) * HEAD_DIM)
                o, st = step(q_ref[0, rows, lanes].astype(F32), k_ref[0, rows, lanes].astype(F32),
                             v_ref[0, rows, lanes].astype(F32), lc_ref[slot, :, lanes], st_ref[h])
                st_ref[h] = st
                o_ref[0, rows, lanes] = (_head_norm(o, gn) * sg_ref[0, rows, lanes].astype(F32)).astype(o_ref.dtype)
            prepare(jnp.minimum(n + 1, n_chunks - 1), 1 - slot)

        @pl.when(mild)
        def _():
            run(lambda q, k, v, g, st: _hgrn_chunk_mild(q, k, v, g, st, causal))

        @pl.when(jnp.logical_not(mild))
        def _():
            run(lambda q, k, v, g, st: _hgrn_chunk(q, k, v, g, st, consts, chunk, sub))

        return carry

    lax.fori_loop(0, n_chunks, body, 0)

    @pl.when(j == pl.num_programs(2) - 1)
    def _():
        for h in range(heads):
            s_ref[0, h] = st_ref[h].T


def _hgrn_prompt(q, k, v, lf, sg, gnorm, s0):
    bsz, t, d = q.shape
    tb = HGRN_BLOCK
    hg = HGRN_HEAD_GROUP
    blk = pl.BlockSpec((1, tb, hg * HEAD_DIM), lambda b, g, j: (b, j, g))
    sblk = pl.BlockSpec((1, hg, HEAD_DIM, HEAD_DIM), lambda b, g, j: (b, g, 0, 0))
    return pl.pallas_call(
        functools.partial(_hgrn_prompt_kernel, chunk=HGRN_CHUNK, sub=HGRN_SUB),
        grid=(bsz, HGRN_HEADS // hg, t // tb),
        in_specs=[blk, blk, blk, blk, blk, _full((1, HEAD_DIM)), sblk],
        out_specs=[blk, sblk],
        out_shape=[jax.ShapeDtypeStruct((bsz, t, d), BF16),
                   jax.ShapeDtypeStruct(s0.shape, F32)],
        scratch_shapes=[pltpu.VMEM((hg, HEAD_DIM, HEAD_DIM), F32),
                        pltpu.VMEM((2, HGRN_CHUNK, hg * HEAD_DIM), F32), pltpu.SMEM((2,), jnp.int32)],
        compiler_params=_params(("arbitrary", "arbitrary", "arbitrary")),
        name="hgrn_prompt",
    )(q, k, v, lf, sg, gnorm, s0)


def _hgrn_sample_kernel(q_ref, k_ref, v_ref, lf_ref, sg_ref, gn_ref, s0_ref, o_ref, s_ref):
    sb, t_new, d_model = q_ref.shape
    gn = gn_ref[...]
    pad_rows = SAMPLE_PAD - t_new - 1

    def body(b, carry):
        q, k, v, sg = q_ref[b], k_ref[b], v_ref[b], sg_ref[b]
        lf = lf_ref[b]
        lrow = []
        for t in range(t_new):
            lrow.append(lf[t:t + 1, :] if t == 0 else lrow[-1] + lf[t:t + 1, :])
        lcum = jnp.concatenate(lrow, axis=0)
        l_end = lrow[-1]
        qe = q * jnp.exp(lcum)
        ke = k * jnp.exp(l_end - lcum)
        f_end = jnp.exp(l_end)
        zeros = jnp.zeros((pad_rows, d_model), F32)
        zero_row = jnp.zeros((1, d_model), F32)
        qe_p = jnp.concatenate([qe, zero_row, zeros], axis=0)
        ke_p = jnp.concatenate([ke, f_end, zeros], axis=0)
        v_p = jnp.concatenate([v, zero_row, zeros], axis=0)
        for h in range(HGRN_HEADS):
            lanes = slice(h * HEAD_DIM, (h + 1) * HEAD_DIM)
            s0 = s0_ref[b, h]
            o = _dot(qe_p[:, lanes].astype(BF16), s0.astype(BF16))[0:t_new]
            rows = []
            for t in range(t_new):
                ot = None
                for s in range(t + 1):
                    prod = q[t:t + 1, lanes] * k[s:s + 1, lanes]
                    if s < t:
                        prod = prod * jnp.exp(lrow[t][:, lanes] - lrow[s][:, lanes])
                    term = jnp.sum(prod, axis=-1, keepdims=True) * v[s:s + 1, lanes]
                    ot = term if ot is None else ot + term
                rows.append(ot)
            o = o + jnp.concatenate(rows, axis=0)
            ke_t = ke_p[:, lanes].T
            decay = ke_t[:, t_new:t_new + 1]
            s_ref[b, h] = s0 * decay + _dot(ke_t.astype(BF16), v_p[:, lanes].astype(BF16))
            o_ref[b, :, lanes] = _head_norm(o, gn) * sg[:, lanes]
        return carry

    lax.fori_loop(0, sb, body, 0)


def _hgrn_sample(q, k, v, lf, sg, gnorm, s0):
    bsz, t, d = q.shape
    sb = HGRN_SAMPLE_BLOCK
    blk = pl.BlockSpec((sb, t, d), lambda g: (g, 0, 0))
    sblk = pl.BlockSpec((sb, HGRN_HEADS, HEAD_DIM, HEAD_DIM), lambda g: (g, 0, 0, 0))
    return pl.pallas_call(
        _hgrn_sample_kernel,
        grid=(bsz // sb,),
        in_specs=[blk, blk, blk, blk, blk, _full((1, HEAD_DIM)), sblk],
        out_specs=[blk, sblk],
        out_shape=[jax.ShapeDtypeStruct((bsz, t, d), F32),
                   jax.ShapeDtypeStruct(s0.shape, F32)],
        compiler_params=_params(("arbitrary",)),
        name="hgrn_sample",
    )(q, k, v, lf, sg, gnorm, s0)


def _merge_kernel(x_ref, ha_ref, hb_ref, ga_ref, gb_ref, wa_ref, wb_ref, wo_ref, g_ref, b_ref, o_ref):
    ya = _dot(ha_ref[...].astype(BF16), wa_ref[...])
    yb = _dot(hb_ref[...].astype(BF16), wb_ref[...])
    mixed = ga_ref[...].astype(F32) * ya + gb_ref[...].astype(F32) * yb
    z = DEEPNORM_ALPHA * x_ref[...] + _dot(mixed.astype(BF16), wo_ref[...])
    o_ref[...] = _layer_norm(z, g_ref[...], b_ref[...])


def _merge(x2d, ha, hb, ga, gb, wa, wb, wo, g, b):
    n, d = x2d.shape
    tm = TOKEN_BLOCK
    row = pl.BlockSpec((tm, d), lambda i: (i, 0))
    return pl.pallas_call(
        _merge_kernel,
        grid=(n // tm,),
        in_specs=[row] * 5 + [_full((d, d))] * 3 + [_full((1, d))] * 2,
        out_specs=row,
        out_shape=jax.ShapeDtypeStruct((n, d), F32),
        compiler_params=_params(("arbitrary",)),
        name="merge",
    )(x2d, ha, hb, ga, gb, wa, wb, wo, g, b)


def _route_t(s, bias):
    n = s.shape[1]
    neg = -jnp.inf
    sb = (s + bias).reshape(N_GROUPS, GROUP_SIZE, n)
    s3 = s.reshape(N_GROUPS, GROUP_SIZE, n)
    e_in_g = lax.broadcasted_iota(jnp.int32, sb.shape, 1)
    m1 = jnp.max(sb, axis=1, keepdims=True)
    first = jnp.min(jnp.where(sb == m1, e_in_g, GROUP_SIZE), axis=1, keepdims=True)
    m2 = jnp.max(jnp.where(e_in_g == first, neg, sb), axis=1, keepdims=True)
    gscore = (m1 + m2)[:, 0, :]
    gid = lax.broadcasted_iota(jnp.int32, gscore.shape, 0)
    gsel = jnp.zeros(gscore.shape, F32)
    for _ in range(TOPK_GROUPS):
        gm = jnp.max(gscore, axis=0, keepdims=True)
        pick = jnp.min(jnp.where(gscore == gm, gid, N_GROUPS), axis=0, keepdims=True)
        hit = gid == pick
        gsel = jnp.where(hit, 1.0, gsel)
        gscore = jnp.where(hit, neg, gscore)
    cand = jnp.where(gsel[:, None, :] > 0.5, sb, neg)
    eid = lax.broadcasted_iota(jnp.int32, sb.shape, 0) * GROUP_SIZE + e_in_g
    esel = jnp.zeros(sb.shape, F32)
    for _ in range(TOP_K):
        em = jnp.max(jnp.max(cand, axis=1, keepdims=True), axis=0, keepdims=True)
        masked = jnp.where(cand == em, eid, N_EXPERTS)
        pick = jnp.min(jnp.min(masked, axis=1, keepdims=True), axis=0, keepdims=True)
        hit = eid == pick
        esel = jnp.where(hit, 1.0, esel)
        cand = jnp.where(hit, neg, cand)
    w = esel * s3
    tot = jnp.sum(jnp.sum(w, axis=1, keepdims=True), axis=0, keepdims=True)
    return (w / tot * ROUTED_SCALE).reshape(N_EXPERTS, n)


def _moe_kernel(x_ref, p_ref, wrt_ref, rb_ref, wg_ref, wu_ref, wd_ref, sg_ref, su_ref, sd_ref,
                g2_ref, b2_ref, pg_ref, pp_ref, o_ref, xb_ref, gt_ref, acc_ref):
    j = pl.program_id(1)
    eb = wg_ref.shape[0]

    @pl.when(j == 0)
    def _():
        x = x_ref[...]
        logits = lax.dot_general(wrt_ref[...], x, (((1,), (1,)), ((), ())),
                                 precision=lax.Precision.HIGHEST, preferred_element_type=F32)
        gt_ref[...] = _route_t(_sigmoid(logits), rb_ref[...])
        xb = x.astype(BF16)
        xb_ref[...] = xb
        hs = _silu(_dot(xb, sg_ref[...])) * _dot(xb, su_ref[...])
        acc_ref[...] = _dot(hs.astype(BF16), sd_ref[...])

    xb = xb_ref[...]
    first = j * eb
    slab = lax.shift_right_logical(first, 3) * SUBLANES
    gates = gt_ref[pl.ds(pl.multiple_of(slab, SUBLANES), SUBLANES), :].T
    in_slab = first - slab
    for r in range(eb):
        gate = gates[:, r:r + 1]
        for off in range(eb, SUBLANES, eb):
            gate = jnp.where(in_slab == off, gates[:, off + r:off + r + 1], gate)
        h = _silu(_dot(xb, wg_ref[r])) * _dot(xb, wu_ref[r])
        acc_ref[...] += _dot((h * gate).astype(BF16), wd_ref[r])

    @pl.when(j == pl.num_programs(1) - 1)
    def _():
        x2 = _layer_norm(DEEPNORM_ALPHA * x_ref[...] + acc_ref[...], g2_ref[...], b2_ref[...])
        gate = _sigmoid(_dot(x2.astype(BF16), pg_ref[...]))
        o_ref[...] = x2 + gate * _dot(p_ref[...].astype(BF16), pp_ref[...])


def _moe(x2d, p2d, wrt, rbias, wg, wu, wd, sg, su, sd, g2, b2, pg, pp):
    n, d = x2d.shape
    tm = MOE_TOKEN_BLOCK if n % MOE_TOKEN_BLOCK == 0 else TOKEN_BLOCK
    eb = EXPERTS_PER_STEP
    assert SUBLANES % eb == 0 and n % tm == 0
    dp = p2d.shape[1]
    ff = wg.shape[2]
    sff = sg.shape[1]
    row = pl.BlockSpec((tm, d), lambda i, j: (i, 0))
    return pl.pallas_call(
        _moe_kernel,
        grid=(n // tm, N_EXPERTS // eb),
        in_specs=[row, pl.BlockSpec((tm, dp), lambda i, j: (i, 0)),
                  _full((N_EXPERTS, d)), _full((N_EXPERTS, 1)),
                  pl.BlockSpec((eb, d, ff), lambda i, j: (j, 0, 0)),
                  pl.BlockSpec((eb, d, ff), lambda i, j: (j, 0, 0)),
                  pl.BlockSpec((eb, ff, d), lambda i, j: (j, 0, 0)),
                  _full((d, sff)), _full((d, sff)), _full((sff, d)),
                  _full((1, d)), _full((1, d)), _full((d, d)), _full((dp, d))],
        out_specs=row,
        out_shape=jax.ShapeDtypeStruct((n, d), F32),
        scratch_shapes=[pltpu.VMEM((tm, d), BF16), pltpu.VMEM((N_EXPERTS, tm), F32),
                        pltpu.VMEM((tm, d), F32)],
        compiler_params=_params(("arbitrary", "arbitrary")),
        name="moe",
    )(x2d, p2d, wrt, rbias, wg, wu, wd, sg, su, sd, g2, b2, pg, pp)


def kernel(x_prompt, x_sample, p_prompt, p_sample, state_conv, state_hgrn, w_in, b_in, hgrn_lb, conv_w, conv_b, conv_ln_g, conv_ln_b, w_conv_out, hgrn_norm_g, w_hgrn_out, w_o, ln1_g, ln1_b, w_router, router_bias, w_exp_gate, w_exp_up, w_exp_down, w_sh_gate, w_sh_up, w_sh_down, ln2_g, ln2_b, w_ple_gate, w_ple_proj):
    assert w_in.shape[0] == DEPTH == 1
    bp, tp, d = x_prompt.shape
    bs, ts, _ = x_sample.shape
    nbuf = CONV_WIDTH - 1
    i = 0
    layer = lambda a: a.reshape(a.shape[1:])
    row = lambda a: a.reshape(1, -1)
    bf = lambda a: layer(a).astype(BF16)

    w_in_b, wco, who, wo = bf(w_in), bf(w_conv_out), bf(w_hgrn_out), bf(w_o)
    weg, weu, wed = bf(w_exp_gate), bf(w_exp_up), bf(w_exp_down)
    wsg, wsu, wsd = bf(w_sh_gate), bf(w_sh_up), bf(w_sh_down)
    wpg, wpp = bf(w_ple_gate), bf(w_ple_proj)
    wrt = layer(w_router).T
    rbias = router_bias.reshape(N_EXPERTS, 1)
    gnorm = row(hgrn_norm_g)

    def tail(x2d, p2d, ha, hb, ga, gb):
        x1 = _merge(x2d, ha, hb, ga, gb, wco, who, wo, row(ln1_g), row(ln1_b))
        return _moe(x1, p2d, wrt, rbias, weg, weu, wed, wsg, wsu, wsd, row(ln2_g), row(ln2_b), wpg, wpp)

    xp = x_prompt.reshape(bp * tp, d)
    ha, u_tail, q, k, v, lf, sg, ga, gb = _proj_conv(xp, tp, w_in_b, row(b_in), hgrn_lb, i, layer(conv_w),
                                                     row(conv_b), row(conv_ln_g), row(conv_ln_b))
    seq = lambda a: a.reshape(bp, tp, d)
    hb, hgrn_p = _hgrn_prompt(seq(q), seq(k), seq(v), seq(lf), seq(sg), gnorm,
                              jnp.zeros((bp, HGRN_HEADS, HEAD_DIM, HEAD_DIM), F32))
    y_p = tail(xp, p_prompt.reshape(bp * tp, -1), ha, hb.reshape(bp * tp, d), ga, gb)
    conv_p = u_tail.reshape(bp, CONV_HALO, d)[:, CONV_HALO - nbuf:, :]

    tmaj = lambda a: jnp.swapaxes(a, 0, 1).reshape(ts * bs, -1)
    xs = tmaj(x_sample)
    u, q, k, v, lf, sg, ga, gb = _proj(xs, w_in_b, row(b_in), hgrn_lb, i)
    seq = lambda a: a.reshape(ts, bs, d)
    ha, conv_s = _conv_sample(seq(u), layer(state_conv), layer(conv_w), row(conv_b), row(conv_ln_g),
                              row(conv_ln_b))
    bmaj = lambda a: jnp.swapaxes(seq(a), 0, 1).astype(F32)
    hb, hgrn_s = _hgrn_sample(bmaj(q), bmaj(k), bmaj(v), bmaj(lf), bmaj(sg), gnorm, layer(state_hgrn))
    y_s = tail(xs, tmaj(layer(p_sample)), ha.reshape(ts * bs, d), tmaj(hb), ga, gb)
    y_s = jnp.swapaxes(y_s.reshape(ts, bs, d), 0, 1)

    return (y_p.reshape(bp, tp, d), y_s, conv_p[None], hgrn_p[None], conv_s[None], hgrn_s[None])
```

```python
import functools

import jax
import jax.numpy as jnp
from jax import lax
from jax.experimental import pallas as pl
from jax.experimental.pallas import tpu as pltpu

F32 = jnp.float32
BF16 = jnp.bfloat16

D_MODEL = 1024
CONV_WIDTH = 31
HGRN_HEADS = 8
HEAD_DIM = D_MODEL // HGRN_HEADS
N_EXPERTS = 64
N_GROUPS = 8
GROUP_SIZE = N_EXPERTS // N_GROUPS
TOPK_GROUPS = 4
TOP_K = 8
EXPERT_FF = 256
ROUTED_SCALE = 2.5
LN_EPS = 1e-5
LOG2_E = 1.4426950408889634
NEG_BIG = 1e30
SUBLANES = 8
LANES = 128
DEPTH = 1
DEEPNORM_ALPHA = (2.0 * DEPTH) ** 0.25

V7X_VMEM_BYTES = 64 * 1024 * 1024
VMEM_LIMIT = V7X_VMEM_BYTES - 8 * 1024 * 1024

TOKEN_BLOCK = 512
CONV_ROWS = 128
NORM_ROWS = 64
SAMPLE_SEQ_BLOCK = 32
CONV_HALO = 32
HGRN_BLOCK = 1024
HGRN_HEAD_GROUP = 8
HGRN_CHUNK = 64
HGRN_MILD_LOG2 = 96.0
HGRN_SUB = 8
HGRN_SAMPLE_BLOCK = 8
SAMPLE_PAD = 16
MOE_TOKEN_BLOCK = 1024
MERGE_TOKEN_BLOCK = 1024
EXPERTS_PER_STEP = 4


def _sigmoid(x):
    return 1.0 / (1.0 + jnp.exp(-x))


def _silu(x):
    return x * _sigmoid(x)


def _layer_norm(x, g, b):
    mu = jnp.mean(x, axis=-1, keepdims=True)
    xc = x - mu
    var = jnp.mean(xc * xc, axis=-1, keepdims=True)
    return xc * lax.rsqrt(var + LN_EPS) * g + b


def _dot(a, b):
    return jnp.dot(a, b, preferred_element_type=F32)


def _dot_nt(a, b):
    return lax.dot_general(a, b, (((1,), (1,)), ((), ())), preferred_element_type=F32)


def _full(shape):
    return pl.BlockSpec(shape, lambda *_: (0,) * len(shape))


def _params(sem):
    return pltpu.CompilerParams(dimension_semantics=sem, vmem_limit_bytes=VMEM_LIMIT)


def _proj_columns(x_ref, w_ref, b_ref, lbp_ref, layer, emit_u, q_ref, k_ref, v_ref, lf_ref, sg_ref, ga_ref, gb_ref):
    D = D_MODEL
    xb = x_ref[...].astype(BF16)

    def col(j):
        return _dot(xb, w_ref[:, j * D:(j + 1) * D]) + b_ref[:, j * D:(j + 1) * D]

    emit_u(col(0) * _sigmoid(col(1)))
    hl = lbp_ref[...]
    e = jnp.exp(hl - jnp.max(hl, axis=0, keepdims=True))
    lb = jnp.sum(e[:layer + 1], axis=0, keepdims=True) / jnp.sum(e, axis=0, keepdims=True)
    fz = col(2)
    lf_ref[...] = jnp.log(lb + (1.0 - lb) * _sigmoid(fz))
    k_ref[...] = ((1.0 - lb) * _sigmoid(-fz)).astype(k_ref.dtype)
    v_ref[...] = col(3).astype(v_ref.dtype)
    q_ref[...] = _silu(col(4)).astype(q_ref.dtype)
    sg_ref[...] = _silu(col(5)).astype(sg_ref.dtype)
    ga_ref[...] = _sigmoid(col(6)).astype(ga_ref.dtype)
    gb_ref[...] = _sigmoid(col(7)).astype(gb_ref.dtype)


def _proj_kernel(x_ref, w_ref, b_ref, lbp_ref, u_ref, q_ref, k_ref, v_ref, lf_ref, sg_ref,
                 ga_ref, gb_ref, *, layer):
    def emit_u(u):
        u_ref[...] = u

    _proj_columns(x_ref, w_ref, b_ref, lbp_ref, layer, emit_u, q_ref, k_ref, v_ref, lf_ref, sg_ref, ga_ref, gb_ref)


def _proj(x2d, w_in, b_in, hgrn_lb, layer):
    n, d = x2d.shape
    tm = TOKEN_BLOCK
    cols = w_in.shape[1]
    row = pl.BlockSpec((tm, d), lambda i: (i, 0))
    outs = [jax.ShapeDtypeStruct((n, d), dt) for dt in (F32, BF16, BF16, BF16, F32, BF16, BF16, BF16)]
    return pl.pallas_call(
        functools.partial(_proj_kernel, layer=layer),
        grid=(n // tm,),
        in_specs=[row,
                  pl.BlockSpec((d, cols), lambda i: (0, 0), pipeline_mode=pl.Buffered(1)),
                  _full((1, cols)), _full(hgrn_lb.shape)],
        out_specs=[row] * 8,
        out_shape=outs,
        compiler_params=_params(("arbitrary",)),
        name="proj",
    )(x2d, w_in, b_in, hgrn_lb)


def _conv_tap_groups():
    shift = CONV_HALO - (CONV_WIDTH - 1)
    groups = [[] for _ in range(SUBLANES)]
    for j in range(CONV_WIDTH):
        groups[(j + shift) % SUBLANES].append((j, (j + shift) // SUBLANES))
    return groups


def _conv_rows(ext_ref, cw_ref, cb_ref, y_ref, base, rows):
    for l in range(D_MODEL // LANES):
        lanes = slice(l * LANES, (l + 1) * LANES)
        acc = None
        for res, taps in enumerate(_conv_tap_groups()):
            part = None
            for j, a in taps:
                term = cw_ref[j:j + 1, lanes] * ext_ref[pl.ds(base + SUBLANES * a, rows + SUBLANES), lanes]
                part = term if part is None else part + term
            part = part[res:res + rows, :]
            acc = part if acc is None else acc + part
        y_ref[pl.ds(base, rows), lanes] = acc + cb_ref[:, lanes]


def _proj_conv_kernel(x_ref, w_ref, b_ref, lbp_ref, cw_ref, cb_ref, g_ref, bb_ref,
                      h_ref, tail_ref, q_ref, k_ref, v_ref, lf_ref, sg_ref, ga_ref, gb_ref,
                      ext_ref, y_ref, *, layer):
    tm = x_ref.shape[0]

    @pl.when(pl.program_id(1) == 0)
    def _():
        ext_ref[0:CONV_HALO, :] = jnp.zeros((CONV_HALO, D_MODEL), F32)
        ext_ref[CONV_HALO + tm:CONV_HALO + tm + SUBLANES, :] = jnp.zeros((SUBLANES, D_MODEL), F32)

    def emit_u(u):
        ext_ref[CONV_HALO:CONV_HALO + tm, :] = u
        tail_ref[...] = u[tm - CONV_HALO:tm, :]
        for r in range(tm // CONV_ROWS):
            _conv_rows(ext_ref, cw_ref, cb_ref, y_ref, r * CONV_ROWS, CONV_ROWS)
        for r in range(tm // NORM_ROWS):
            rows = slice(r * NORM_ROWS, (r + 1) * NORM_ROWS)
            h_ref[rows, :] = _silu(_layer_norm(y_ref[rows, :], g_ref[...], bb_ref[...])).astype(h_ref.dtype)
        ext_ref[0:CONV_HALO, :] = ext_ref[tm:tm + CONV_HALO, :]

    _proj_columns(x_ref, w_ref, b_ref, lbp_ref, layer, emit_u, q_ref, k_ref, v_ref, lf_ref, sg_ref, ga_ref, gb_ref)


def _proj_conv(x2d, seq_len, w_in, b_in, hgrn_lb, layer, conv_w, conv_b, g, b):
    n, d = x2d.shape
    tm = TOKEN_BLOCK
    nt = seq_len // tm
    cols = w_in.shape[1]
    row = pl.BlockSpec((tm, d), lambda s, j: (s * nt + j, 0))
    outs = [jax.ShapeDtypeStruct((n, d), BF16), jax.ShapeDtypeStruct((n // seq_len * CONV_HALO, d), F32)]
    outs += [jax.ShapeDtypeStruct((n, d), dt) for dt in (BF16, BF16, BF16, F32, BF16, BF16, BF16)]
    return pl.pallas_call(
        functools.partial(_proj_conv_kernel, layer=layer),
        grid=(n // seq_len, nt),
        in_specs=[row,
                  pl.BlockSpec((d, cols), lambda s, j: (0, 0), pipeline_mode=pl.Buffered(1)),
                  _full((1, cols)), _full(hgrn_lb.shape),
                  _full(conv_w.shape), _full((1, d)), _full((1, d)), _full((1, d))],
        out_specs=[row, pl.BlockSpec((CONV_HALO, d), lambda s, j: (s, 0))] + [row] * 7,
        out_shape=outs,
        scratch_shapes=[pltpu.VMEM((CONV_HALO + tm + SUBLANES, d), F32), pltpu.VMEM((tm, d), F32)],
        compiler_params=_params(("arbitrary", "arbitrary")),
        name="proj_conv",
    )(x2d, w_in, b_in, hgrn_lb, conv_w, conv_b, g, b)


def _conv_sample_kernel(u_ref, st_ref, cw_ref, cb_ref, g_ref, b_ref, h_ref, new_ref):
    t_new, sb, d = u_ref.shape
    nbuf = st_ref.shape[1]

    def ext_row(r, rows):
        if r < nbuf:
            return st_ref[rows, r, :]
        return u_ref[r - nbuf, rows, :]

    def body(gi, carry):
        rows = pl.ds(pl.multiple_of(gi * SUBLANES, SUBLANES), SUBLANES)
        for t in range(t_new):
            acc = jnp.zeros((SUBLANES, d), F32) + cb_ref[...]
            for j in range(CONV_WIDTH):
                acc = acc + cw_ref[j:j + 1, :] * ext_row(t + j, rows)
            h_ref[t, rows, :] = _silu(_layer_norm(acc, g_ref[...], b_ref[...])).astype(h_ref.dtype)
        return carry

    lax.fori_loop(0, sb // SUBLANES, body, 0)
    new_ref[:, 0:nbuf - t_new, :] = st_ref[:, t_new:nbuf, :]
    for t in range(t_new):
        new_ref[:, nbuf - t_new + t, :] = u_ref[t]


def _conv_sample(u, state, conv_w, conv_b, g, b):
    t, bsz, d = u.shape
    nbuf = state.shape[1]
    sb = SAMPLE_SEQ_BLOCK
    return pl.pallas_call(
        _conv_sample_kernel,
        grid=(bsz // sb,),
        in_specs=[pl.BlockSpec((t, sb, d), lambda i: (0, i, 0)),
                  pl.BlockSpec((sb, nbuf, d), lambda i: (i, 0, 0)),
                  _full(conv_w.shape), _full((1, d)), _full((1, d)), _full((1, d))],
        out_specs=[pl.BlockSpec((t, sb, d), lambda i: (0, i, 0)),
                   pl.BlockSpec((sb, nbuf, d), lambda i: (i, 0, 0))],
        out_shape=[jax.ShapeDtypeStruct((t, bsz, d), F32),
                   jax.ShapeDtypeStruct((bsz, nbuf, d), F32)],
        compiler_params=_params(("arbitrary",)),
        name="conv_sample",
    )(u, state, conv_w, conv_b, g, b)


def _hgrn_chunk(q, k, v, g, st, consts, chunk, sub):
    _, blk_mask, neg_masks = consts
    nb = chunk // sub
    lcum = g
    l_end = lcum[chunk - 1:chunk, :]

    o = _dot_nt((q * jnp.exp2(lcum)).astype(BF16), st.astype(BF16))

    if nb > 1:
        starts = [lcum[i * sub - 1:i * sub, :] for i in range(1, nb)]
        l_start = jnp.concatenate(
            [jnp.zeros((sub, HEAD_DIM), F32)]
            + [jnp.broadcast_to(s, (sub, HEAD_DIM)) for s in starts], axis=0)
        q_rel = (q * jnp.exp2(lcum - l_start)).astype(BF16)
        k_stack = jnp.concatenate(
            [k[0:i * sub] * jnp.exp2(starts[i - 1] - lcum[0:i * sub]) for i in range(1, nb)],
            axis=0).astype(BF16)
        v_stack = jnp.concatenate([v[0:i * sub] for i in range(1, nb)], axis=0).astype(BF16)
        scores = _dot_nt(q_rel, k_stack) * blk_mask
        o = o + _dot(scores.astype(BF16), v_stack)

    diag = []
    for i in range(nb):
        sl = slice(i * sub, (i + 1) * sub)
        qi, ki, li = q[sl], k[sl], lcum[sl]
        od = None
        for s in range(sub):
            dec = jnp.exp2(li - li[s:s + 1, :] + neg_masks[s])
            score = jnp.sum(qi * (ki[s:s + 1, :] * dec), axis=-1, keepdims=True)
            term = score * v[i * sub + s:i * sub + s + 1, :]
            od = term if od is None else od + term
        diag.append(od)
    o = o + (jnp.concatenate(diag, axis=0) if nb > 1 else diag[0])

    k_end = (k * jnp.exp2(l_end - lcum)).astype(BF16)
    st_new = st * jnp.exp2(l_end) + _dot(v.T.astype(BF16), k_end)
    return o, st_new


def _hgrn_chunk_mild(q, k, v, g, st, causal):
    chunk = q.shape[0]
    l_end = g[chunk - 1:chunk, :]
    q_dec = (q * jnp.exp2(g)).astype(BF16)
    k_inv = (k * jnp.exp2(-g)).astype(BF16)
    scores = jnp.where(causal, _dot_nt(q_dec, k_inv), 0.0).astype(BF16)
    o = _dot(jnp.concatenate([q_dec, scores], axis=1),
             jnp.concatenate([st.T.astype(BF16), v.astype(BF16)], axis=0))
    st_new = (st + _dot(v.T.astype(BF16), k_inv)) * jnp.exp2(l_end)
    return o, st_new


def _hgrn_consts(chunk, sub):
    nb = chunk // sub
    r = lax.broadcasted_iota(jnp.int32, (chunk, chunk), 0)
    c = lax.broadcasted_iota(jnp.int32, (chunk, chunk), 1)
    tri = jnp.where(c <= r, 1.0, 0.0).astype(BF16)
    row = lax.broadcasted_iota(jnp.int32, (sub, HEAD_DIM), 0)
    neg_masks = [jnp.where(row >= s, 0.0, -NEG_BIG).astype(F32) for s in range(sub)]
    blk_mask = None
    if nb > 1:
        width = sub * nb * (nb - 1) // 2
        rb = lax.broadcasted_iota(jnp.int32, (chunk, width), 0) // sub
        cc = lax.broadcasted_iota(jnp.int32, (chunk, width), 1)
        blk_mask = jnp.zeros((chunk, width), F32)
        off = 0
        for i in range(1, nb):
            hit = jnp.where(rb == i, jnp.where(cc >= off, jnp.where(cc < off + i * sub, 1.0, 0.0), 0.0), 0.0)
            blk_mask = blk_mask + hit
            off += i * sub
    return tri, blk_mask, neg_masks


def _cumsum_rows(tri_bf, g):
    g1 = g.astype(BF16)
    r1 = g - g1.astype(F32)
    g2 = r1.astype(BF16)
    g3 = (r1 - g2.astype(F32)).astype(BF16)
    return _dot(tri_bf, g1) + _dot(tri_bf, g2) + _dot(tri_bf, g3)


def _head_norm(o, gn):
    return o * lax.rsqrt(jnp.mean(o * o, axis=-1, keepdims=True) + LN_EPS) * gn


def _hgrn_prompt_kernel(q_ref, k_ref, v_ref, lf_ref, sg_ref, gn_ref, s0_ref, o_ref, s_ref,
                        st_ref, lc_ref, mild_ref, *, chunk, sub):
    tb = q_ref.shape[1]
    n_chunks = tb // chunk
    heads = q_ref.shape[2] // HEAD_DIM
    j = pl.program_id(2)
    consts = _hgrn_consts(chunk, sub)
    gn = gn_ref[...]

    @pl.when(j == 0)
    def _():
        for h in range(heads):
            st_ref[h] = s0_ref[0, h].T

    causal = (lax.broadcasted_iota(jnp.int32, (chunk, chunk), 1)
              <= lax.broadcasted_iota(jnp.int32, (chunk, chunk), 0))

    def prepare(n, slot):
        rows = pl.ds(pl.multiple_of(n * chunk, chunk), chunk)
        lcum = _cumsum_rows(consts[0], lf_ref[0, rows, :]) * LOG2_E
        lc_ref[slot] = lcum
        mild_ref[slot] = (jnp.min(lcum[chunk - 1:chunk, :]) >= -HGRN_MILD_LOG2).astype(jnp.int32)

    prepare(0, 0)

    def body(n, carry):
        rows = pl.ds(pl.multiple_of(n * chunk, chunk), chunk)
        slot = lax.rem(n, 2)
        mild = mild_ref[slot] == 1

        def run(step):
            for h in range(heads):
                lanes = slice(h * HEAD_DIM, (h + 1) * HEAD_DIM)
                o, st = step(q_ref[0, rows, lanes].astype(F32), k_ref[0, rows, lanes].astype(F32),
                             v_ref[0, rows, lanes].astype(F32), lc_ref[slot, :, lanes], st_ref[h])
                st_ref[h] = st
                o_ref[0, rows, lanes] = (_head_norm(o, gn) * sg_ref[0, rows, lanes].astype(F32)).astype(o_ref.dtype)
            prepare(jnp.minimum(n + 1, n_chunks - 1), 1 - slot)

        @pl.when(mild)
        def _():
            run(lambda q, k, v, g, st: _hgrn_chunk_mild(q, k, v, g, st, causal))

        @pl.when(jnp.logical_not(mild))
        def _():
            run(lambda q, k, v, g, st: _hgrn_chunk(q, k, v, g, st, consts, chunk, sub))

        return carry

    lax.fori_loop(0, n_chunks, body, 0)

    @pl.when(j == pl.num_programs(2) - 1)
    def _():
        for h in range(heads):
            s_ref[0, h] = st_ref[h].T


def _hgrn_prompt(q, k, v, lf, sg, gnorm, s0):
    bsz, t, d = q.shape
    tb = HGRN_BLOCK
    hg = HGRN_HEAD_GROUP
    blk = pl.BlockSpec((1, tb, hg * HEAD_DIM), lambda b, g, j: (b, j, g))
    sblk = pl.BlockSpec((1, hg, HEAD_DIM, HEAD_DIM), lambda b, g, j: (b, g, 0, 0))
    return pl.pallas_call(
        functools.partial(_hgrn_prompt_kernel, chunk=HGRN_CHUNK, sub=HGRN_SUB),
        grid=(bsz, HGRN_HEADS // hg, t // tb),
        in_specs=[blk, blk, blk, blk, blk, _full((1, HEAD_DIM)), sblk],
        out_specs=[blk, sblk],
        out_shape=[jax.ShapeDtypeStruct((bsz, t, d), BF16),
                   jax.ShapeDtypeStruct(s0.shape, F32)],
        scratch_shapes=[pltpu.VMEM((hg, HEAD_DIM, HEAD_DIM), F32),
                        pltpu.VMEM((2, HGRN_CHUNK, hg * HEAD_DIM), F32), pltpu.SMEM((2,), jnp.int32)],
        compiler_params=_params(("arbitrary", "arbitrary", "arbitrary")),
        name="hgrn_prompt",
    )(q, k, v, lf, sg, gnorm, s0)


def _hgrn_sample_kernel(q_ref, k_ref, v_ref, lf_ref, sg_ref, gn_ref, s0_ref, o_ref, s_ref):
    sb, t_new, d_model = q_ref.shape
    gn = gn_ref[...]
    pad_rows = SAMPLE_PAD - t_new - 1

    def body(b, carry):
        q, k, v, sg = q_ref[b], k_ref[b], v_ref[b], sg_ref[b]
        lf = lf_ref[b]
        lrow = []
        for t in range(t_new):
            lrow.append(lf[t:t + 1, :] if t == 0 else lrow[-1] + lf[t:t + 1, :])
        lcum = jnp.concatenate(lrow, axis=0)
        l_end = lrow[-1]
        qe = q * jnp.exp(lcum)
        ke = k * jnp.exp(l_end - lcum)
        f_end = jnp.exp(l_end)
        zeros = jnp.zeros((pad_rows, d_model), F32)
        zero_row = jnp.zeros((1, d_model), F32)
        qe_p = jnp.concatenate([qe, zero_row, zeros], axis=0)
        ke_p = jnp.concatenate([ke, f_end, zeros], axis=0)
        v_p = jnp.concatenate([v, zero_row, zeros], axis=0)
        for h in range(HGRN_HEADS):
            lanes = slice(h * HEAD_DIM, (h + 1) * HEAD_DIM)
            s0 = s0_ref[b, h]
            o = _dot(qe_p[:, lanes].astype(BF16), s0.astype(BF16))[0:t_new]
            rows = []
            for t in range(t_new):
                ot = None
                for s in range(t + 1):
                    prod = q[t:t + 1, lanes] * k[s:s + 1, lanes]
                    if s < t:
                        prod = prod * jnp.exp(lrow[t][:, lanes] - lrow[s][:, lanes])
                    term = jnp.sum(prod, axis=-1, keepdims=True) * v[s:s + 1, lanes]
                    ot = term if ot is None else ot + term
                rows.append(ot)
            o = o + jnp.concatenate(rows, axis=0)
            ke_t = ke_p[:, lanes].T
            decay = ke_t[:, t_new:t_new + 1]
            s_ref[b, h] = s0 * decay + _dot(ke_t.astype(BF16), v_p[:, lanes].astype(BF16))
            o_ref[b, :, lanes] = _head_norm(o, gn) * sg[:, lanes]
        return carry

    lax.fori_loop(0, sb, body, 0)


def _hgrn_sample(q, k, v, lf, sg, gnorm, s0):
    bsz, t, d = q.shape
    sb = HGRN_SAMPLE_BLOCK
    blk = pl.BlockSpec((sb, t, d), lambda g: (g, 0, 0))
    sblk = pl.BlockSpec((sb, HGRN_HEADS, HEAD_DIM, HEAD_DIM), lambda g: (g, 0, 0, 0))
    return pl.pallas_call(
        _hgrn_sample_kernel,
        grid=(bsz // sb,),
        in_specs=[blk, blk, blk, blk, blk, _full((1, HEAD_DIM)), sblk],
        out_specs=[blk, sblk],
        out_shape=[jax.ShapeDtypeStruct((bsz, t, d), F32),
                   jax.ShapeDtypeStruct(s0.shape, F32)],
        compiler_params=_params(("arbitrary",)),
        name="hgrn_sample",
    )(q, k, v, lf, sg, gnorm, s0)


def _merge_kernel(x_ref, ha_ref, hb_ref, ga_ref, gb_ref, wa_ref, wb_ref, wo_ref, g_ref, b_ref, o_ref):
    ya = _dot(ha_ref[...].astype(BF16), wa_ref[...])
    yb = _dot(hb_ref[...].astype(BF16), wb_ref[...])
    mixed = ga_ref[...].astype(F32) * ya + gb_ref[...].astype(F32) * yb
    z = DEEPNORM_ALPHA * x_ref[...] + _dot(mixed.astype(BF16), wo_ref[...])
    o_ref[...] = _layer_norm(z, g_ref[...], b_ref[...])


def _merge(x2d, ha, hb, ga, gb, wa, wb, wo, g, b):
    n, d = x2d.shape
    tm = MERGE_TOKEN_BLOCK if n % MERGE_TOKEN_BLOCK == 0 else TOKEN_BLOCK
    row = pl.BlockSpec((tm, d), lambda i: (i, 0))
    return pl.pallas_call(
        _merge_kernel,
        grid=(n // tm,),
        in_specs=[row] * 5 + [_full((d, d))] * 3 + [_full((1, d))] * 2,
        out_specs=row,
        out_shape=jax.ShapeDtypeStruct((n, d), F32),
        compiler_params=_params(("arbitrary",)),
        name="merge",
    )(x2d, ha, hb, ga, gb, wa, wb, wo, g, b)


def _route_t(s, bias):
    n = s.shape[1]
    neg = -jnp.inf
    sb = (s + bias).reshape(N_GROUPS, GROUP_SIZE, n)
    s3 = s.reshape(N_GROUPS, GROUP_SIZE, n)
    e_in_g = lax.broadcasted_iota(jnp.int32, sb.shape, 1)
    m1 = jnp.max(sb, axis=1, keepdims=True)
    first = jnp.min(jnp.where(sb == m1, e_in_g, GROUP_SIZE), axis=1, keepdims=True)
    m2 = jnp.max(jnp.where(e_in_g == first, neg, sb), axis=1, keepdims=True)
    gscore = (m1 + m2)[:, 0, :]
    gid = lax.broadcasted_iota(jnp.int32, gscore.shape, 0)
    gsel = jnp.zeros(gscore.shape, F32)
    for _ in range(TOPK_GROUPS):
        gm = jnp.max(gscore, axis=0, keepdims=True)
        pick = jnp.min(jnp.where(gscore == gm, gid, N_GROUPS), axis=0, keepdims=True)
        hit = gid == pick
        gsel = jnp.where(hit, 1.0, gsel)
        gscore = jnp.where(hit, neg, gscore)
    cand = jnp.where(gsel[:, None, :] > 0.5, sb, neg)
    eid = lax.broadcasted_iota(jnp.int32, sb.shape, 0) * GROUP_SIZE + e_in_g
    esel = jnp.zeros(sb.shape, F32)
    for _ in range(TOP_K):
        em = jnp.max(jnp.max(cand, axis=1, keepdims=True), axis=0, keepdims=True)
        masked = jnp.where(cand == em, eid, N_EXPERTS)
        pick = jnp.min(jnp.min(masked, axis=1, keepdims=True), axis=0, keepdims=True)
        hit = eid == pick
        esel = jnp.where(hit, 1.0, esel)
        cand = jnp.where(hit, neg, cand)
    w = esel * s3
    tot = jnp.sum(jnp.sum(w, axis=1, keepdims=True), axis=0, keepdims=True)
    return (w / tot * ROUTED_SCALE).reshape(N_EXPERTS, n)


def _moe_kernel(x_ref, p_ref, wrt_ref, rb_ref, wg_ref, wu_ref, wd_ref, sg_ref, su_ref, sd_ref,
                g2_ref, b2_ref, pg_ref, pp_ref, o_ref, xb_ref, gt_ref, acc_ref):
    j = pl.program_id(1)
    eb = wg_ref.shape[0]

    @pl.when(j == 0)
    def _():
        x = x_ref[...]
        logits = lax.dot_general(wrt_ref[...], x, (((1,), (1,)), ((), ())),
                                 precision=lax.Precision.HIGHEST, preferred_element_type=F32)
        gt_ref[...] = _route_t(_sigmoid(logits), rb_ref[...])
        xb = x.astype(BF16)
        xb_ref[...] = xb
        hs = _silu(_dot(xb, sg_ref[...])) * _dot(xb, su_ref[...])
        acc_ref[...] = _dot(hs.astype(BF16), sd_ref[...])

    xb = xb_ref[...]
    first = j * eb
    slab = lax.shift_right_logical(first, 3) * SUBLANES
    gates = gt_ref[pl.ds(pl.multiple_of(slab, SUBLANES), SUBLANES), :].T
    in_slab = first - slab
    for r in range(eb):
        gate = gates[:, r:r + 1]
        for off in range(eb, SUBLANES, eb):
            gate = jnp.where(in_slab == off, gates[:, off + r:off + r + 1], gate)
        h = _silu(_dot(xb, wg_ref[r])) * _dot(xb, wu_ref[r])
        acc_ref[...] += _dot((h * gate).astype(BF16), wd_ref[r])

    @pl.when(j == pl.num_programs(1) - 1)
    def _():
        x2 = _layer_norm(DEEPNORM_ALPHA * x_ref[...] + acc_ref[...], g2_ref[...], b2_ref[...])
        gate = _sigmoid(_dot(x2.astype(BF16), pg_ref[...]))
        o_ref[...] = x2 + gate * _dot(p_ref[...].astype(BF16), pp_ref[...])


def _moe(x2d, p2d, wrt, rbias, wg, wu, wd, sg, su, sd, g2, b2, pg, pp):
    n, d = x2d.shape
    tm = MOE_TOKEN_BLOCK if n % MOE_TOKEN_BLOCK == 0 else TOKEN_BLOCK
    eb = EXPERTS_PER_STEP if n >= 2 * MOE_TOKEN_BLOCK else SUBLANES
    assert SUBLANES % eb == 0 and n % tm == 0
    dp = p2d.shape[1]
    ff = wg.shape[2]
    sff = sg.shape[1]
    row = pl.BlockSpec((tm, d), lambda i, j: (i, 0))
    return pl.pallas_call(
        _moe_kernel,
        grid=(n // tm, N_EXPERTS // eb),
        in_specs=[row, pl.BlockSpec((tm, dp), lambda i, j: (i, 0)),
                  _full((N_EXPERTS, d)), _full((N_EXPERTS, 1)),
                  pl.BlockSpec((eb, d, ff), lambda i, j: (j, 0, 0)),
                  pl.BlockSpec((eb, d, ff), lambda i, j: (j, 0, 0)),
                  pl.BlockSpec((eb, ff, d), lambda i, j: (j, 0, 0)),
                  _full((d, sff)), _full((d, sff)), _full((sff, d)),
                  _full((1, d)), _full((1, d)), _full((d, d)), _full((dp, d))],
        out_specs=row,
        out_shape=jax.ShapeDtypeStruct((n, d), F32),
        scratch_shapes=[pltpu.VMEM((tm, d), BF16), pltpu.VMEM((N_EXPERTS, tm), F32),
                        pltpu.VMEM((tm, d), F32)],
        compiler_params=_params(("arbitrary", "arbitrary")),
        name="moe",
    )(x2d, p2d, wrt, rbias, wg, wu, wd, sg, su, sd, g2, b2, pg, pp)


def kernel(x_prompt, x_sample, p_prompt, p_sample, state_conv, state_hgrn, w_in, b_in, hgrn_lb, conv_w, conv_b, conv_ln_g, conv_ln_b, w_conv_out, hgrn_norm_g, w_hgrn_out, w_o, ln1_g, ln1_b, w_router, router_bias, w_exp_gate, w_exp_up, w_exp_down, w_sh_gate, w_sh_up, w_sh_down, ln2_g, ln2_b, w_ple_gate, w_ple_proj):
    assert w_in.shape[0] == DEPTH == 1
    bp, tp, d = x_prompt.shape
    bs, ts, _ = x_sample.shape
    nbuf = CONV_WIDTH - 1
    i = 0
    layer = lambda a: a.reshape(a.shape[1:])
    row = lambda a: a.reshape(1, -1)
    bf = lambda a: layer(a).astype(BF16)

    w_in_b, wco, who, wo = bf(w_in), bf(w_conv_out), bf(w_hgrn_out), bf(w_o)
    weg, weu, wed = bf(w_exp_gate), bf(w_exp_up), bf(w_exp_down)
    wsg, wsu, wsd = bf(w_sh_gate), bf(w_sh_up), bf(w_sh_down)
    wpg, wpp = bf(w_ple_gate), bf(w_ple_proj)
    wrt = layer(w_router).T
    rbias = router_bias.reshape(N_EXPERTS, 1)
    gnorm = row(hgrn_norm_g)

    def tail(x2d, p2d, ha, hb, ga, gb):
        x1 = _merge(x2d, ha, hb, ga, gb, wco, who, wo, row(ln1_g), row(ln1_b))
        return _moe(x1, p2d, wrt, rbias, weg, weu, wed, wsg, wsu, wsd, row(ln2_g), row(ln2_b), wpg, wpp)

    xp = x_prompt.reshape(bp * tp, d)
    ha, u_tail, q, k, v, lf, sg, ga, gb = _proj_conv(xp, tp, w_in_b, row(b_in), hgrn_lb, i, layer(conv_w),
                                                     row(conv_b), row(conv_ln_g), row(conv_ln_b))
    seq = lambda a: a.reshape(bp, tp, d)
    hb, hgrn_p = _hgrn_prompt(seq(q), seq(k), seq(v), seq(lf), seq(sg), gnorm,
                              jnp.zeros((bp, HGRN_HEADS, HEAD_DIM, HEAD_DIM), F32))
    y_p = tail(xp, p_prompt.reshape(bp * tp, -1), ha, hb.reshape(bp * tp, d), ga, gb)
    conv_p = u_tail.reshape(bp, CONV_HALO, d)[:, CONV_HALO - nbuf:, :]

    tmaj = lambda a: jnp.swapaxes(a, 0, 1).reshape(ts * bs, -1)
    xs = tmaj(x_sample)
    u, q, k, v, lf, sg, ga, gb = _proj(xs, w_in_b, row(b_in), hgrn_lb, i)
    seq = lambda a: a.reshape(ts, bs, d)
    ha, conv_s = _conv_sample(seq(u), layer(state_conv), layer(conv_w), row(conv_b), row(conv_ln_g),
                              row(conv_ln_b))
    bmaj = lambda a: jnp.swapaxes(seq(a), 0, 1).astype(F32)
    hb, hgrn_s = _hgrn_sample(bmaj(q), bmaj(k), bmaj(v), bmaj(lf), bmaj(sg), gnorm, layer(state_hgrn))
    y_s = tail(xs, tmaj(layer(p_sample)), ha.reshape(ts * bs, d), tmaj(hb), ga, gb)
    y_s = jnp.swapaxes(y_s.reshape(ts, bs, d), 0, 1)

    return (y_p.reshape(bp, tp, d), y_s, conv_p[None], hgrn_p[None], conv_s[None], hgrn_s[None])
```

```python
import functools

import jax
import jax.numpy as jnp
from jax import lax
from jax.experimental import pallas as pl
from jax.experimental.pallas import tpu as pltpu

F32 = jnp.float32
BF16 = jnp.bfloat16

D_MODEL = 1024
CONV_WIDTH = 31
HGRN_HEADS = 8
HEAD_DIM = D_MODEL // HGRN_HEADS
N_EXPERTS = 64
N_GROUPS = 8
GROUP_SIZE = N_EXPERTS // N_GROUPS
TOPK_GROUPS = 4
TOP_K = 8
EXPERT_FF = 256
ROUTED_SCALE = 2.5
LN_EPS = 1e-5
LOG2_E = 1.4426950408889634
NEG_BIG = 1e30
SUBLANES = 8
LANES = 128
DEPTH = 1
DEEPNORM_ALPHA = (2.0 * DEPTH) ** 0.25

V7X_VMEM_BYTES = 64 * 1024 * 1024
VMEM_LIMIT = V7X_VMEM_BYTES - 8 * 1024 * 1024

TOKEN_BLOCK = 512
CONV_ROWS = 128
NORM_ROWS = 64
SAMPLE_SEQ_BLOCK = 32
CONV_HALO = 32
HGRN_BLOCK = 1024
HGRN_HEAD_GROUP = 8
HGRN_CHUNK = 64
HGRN_MILD_LOG2 = 96.0
HGRN_SUB = 8
HGRN_SAMPLE_BLOCK = 8
SAMPLE_PAD = 16
MOE_TOKEN_BLOCK = 512
MOE_GROUP_CAP = 288
MERGE_TOKEN_BLOCK = 1024
EXPERTS_PER_STEP = 4


def _sigmoid(x):
    return 1.0 / (1.0 + jnp.exp(-x))


def _silu(x):
    return x * _sigmoid(x)


def _layer_norm(x, g, b):
    mu = jnp.mean(x, axis=-1, keepdims=True)
    xc = x - mu
    var = jnp.mean(xc * xc, axis=-1, keepdims=True)
    return xc * lax.rsqrt(var + LN_EPS) * g + b


def _dot(a, b):
    return jnp.dot(a, b, preferred_element_type=F32)


def _dot_nt(a, b):
    return lax.dot_general(a, b, (((1,), (1,)), ((), ())), preferred_element_type=F32)


def _full(shape):
    return pl.BlockSpec(shape, lambda *_: (0,) * len(shape))


def _params(sem):
    return pltpu.CompilerParams(dimension_semantics=sem, vmem_limit_bytes=VMEM_LIMIT)


def _proj_columns(x_ref, w_ref, b_ref, lbp_ref, layer, emit_u, q_ref, k_ref, v_ref, lf_ref, sg_ref, ga_ref, gb_ref):
    D = D_MODEL
    xb = x_ref[...].astype(BF16)

    def col(j):
        return _dot(xb, w_ref[:, j * D:(j + 1) * D]) + b_ref[:, j * D:(j + 1) * D]

    emit_u(col(0) * _sigmoid(col(1)))
    hl = lbp_ref[...]
    e = jnp.exp(hl - jnp.max(hl, axis=0, keepdims=True))
    lb = jnp.sum(e[:layer + 1], axis=0, keepdims=True) / jnp.sum(e, axis=0, keepdims=True)
    fz = col(2)
    lf_ref[...] = jnp.log(lb + (1.0 - lb) * _sigmoid(fz))
    k_ref[...] = ((1.0 - lb) * _sigmoid(-fz)).astype(k_ref.dtype)
    v_ref[...] = col(3).astype(v_ref.dtype)
    q_ref[...] = _silu(col(4)).astype(q_ref.dtype)
    sg_ref[...] = _silu(col(5)).astype(sg_ref.dtype)
    ga_ref[...] = _sigmoid(col(6)).astype(ga_ref.dtype)
    gb_ref[...] = _sigmoid(col(7)).astype(gb_ref.dtype)


def _proj_kernel(x_ref, w_ref, b_ref, lbp_ref, u_ref, q_ref, k_ref, v_ref, lf_ref, sg_ref,
                 ga_ref, gb_ref, *, layer):
    def emit_u(u):
        u_ref[...] = u

    _proj_columns(x_ref, w_ref, b_ref, lbp_ref, layer, emit_u, q_ref, k_ref, v_ref, lf_ref, sg_ref, ga_ref, gb_ref)


def _proj(x2d, w_in, b_in, hgrn_lb, layer):
    n, d = x2d.shape
    tm = TOKEN_BLOCK
    cols = w_in.shape[1]
    row = pl.BlockSpec((tm, d), lambda i: (i, 0))
    outs = [jax.ShapeDtypeStruct((n, d), dt) for dt in (F32, BF16, BF16, BF16, F32, BF16, BF16, BF16)]
    return pl.pallas_call(
        functools.partial(_proj_kernel, layer=layer),
        grid=(n // tm,),
        in_specs=[row,
                  pl.BlockSpec((d, cols), lambda i: (0, 0), pipeline_mode=pl.Buffered(1)),
                  _full((1, cols)), _full(hgrn_lb.shape)],
        out_specs=[row] * 8,
        out_shape=outs,
        compiler_params=_params(("arbitrary",)),
        name="proj",
    )(x2d, w_in, b_in, hgrn_lb)


def _conv_tap_groups():
    shift = CONV_HALO - (CONV_WIDTH - 1)
    groups = [[] for _ in range(SUBLANES)]
    for j in range(CONV_WIDTH):
        groups[(j + shift) % SUBLANES].append((j, (j + shift) // SUBLANES))
    return groups


def _conv_rows(ext_ref, cw_ref, cb_ref, y_ref, base, rows):
    for l in range(D_MODEL // LANES):
        lanes = slice(l * LANES, (l + 1) * LANES)
        acc = None
        for res, taps in enumerate(_conv_tap_groups()):
            part = None
            for j, a in taps:
                term = cw_ref[j:j + 1, lanes] * ext_ref[pl.ds(base + SUBLANES * a, rows + SUBLANES), lanes]
                part = term if part is None else part + term
            part = part[res:res + rows, :]
            acc = part if acc is None else acc + part
        y_ref[pl.ds(base, rows), lanes] = acc + cb_ref[:, lanes]


def _proj_conv_kernel(x_ref, w_ref, b_ref, lbp_ref, cw_ref, cb_ref, g_ref, bb_ref,
                      h_ref, tail_ref, q_ref, k_ref, v_ref, lf_ref, sg_ref, ga_ref, gb_ref,
                      ext_ref, y_ref, *, layer):
    tm = x_ref.shape[0]

    @pl.when(pl.program_id(1) == 0)
    def _():
        ext_ref[0:CONV_HALO, :] = jnp.zeros((CONV_HALO, D_MODEL), F32)
        ext_ref[CONV_HALO + tm:CONV_HALO + tm + SUBLANES, :] = jnp.zeros((SUBLANES, D_MODEL), F32)

    def emit_u(u):
        ext_ref[CONV_HALO:CONV_HALO + tm, :] = u
        tail_ref[...] = u[tm - CONV_HALO:tm, :]
        for r in range(tm // CONV_ROWS):
            _conv_rows(ext_ref, cw_ref, cb_ref, y_ref, r * CONV_ROWS, CONV_ROWS)
        for r in range(tm // NORM_ROWS):
            rows = slice(r * NORM_ROWS, (r + 1) * NORM_ROWS)
            h_ref[rows, :] = _silu(_layer_norm(y_ref[rows, :], g_ref[...], bb_ref[...])).astype(h_ref.dtype)
        ext_ref[0:CONV_HALO, :] = ext_ref[tm:tm + CONV_HALO, :]

    _proj_columns(x_ref, w_ref, b_ref, lbp_ref, layer, emit_u, q_ref, k_ref, v_ref, lf_ref, sg_ref, ga_ref, gb_ref)


def _proj_conv(x2d, seq_len, w_in, b_in, hgrn_lb, layer, conv_w, conv_b, g, b):
    n, d = x2d.shape
    tm = TOKEN_BLOCK
    nt = seq_len // tm
    cols = w_in.shape[1]
    row = pl.BlockSpec((tm, d), lambda s, j: (s * nt + j, 0))
    outs = [jax.ShapeDtypeStruct((n, d), BF16), jax.ShapeDtypeStruct((n // seq_len * CONV_HALO, d), F32)]
    outs += [jax.ShapeDtypeStruct((n, d), dt) for dt in (BF16, BF16, BF16, F32, BF16, BF16, BF16)]
    return pl.pallas_call(
        functools.partial(_proj_conv_kernel, layer=layer),
        grid=(n // seq_len, nt),
        in_specs=[row,
                  pl.BlockSpec((d, cols), lambda s, j: (0, 0), pipeline_mode=pl.Buffered(1)),
                  _full((1, cols)), _full(hgrn_lb.shape),
                  _full(conv_w.shape), _full((1, d)), _full((1, d)), _full((1, d))],
        out_specs=[row, pl.BlockSpec((CONV_HALO, d), lambda s, j: (s, 0))] + [row] * 7,
        out_shape=outs,
        scratch_shapes=[pltpu.VMEM((CONV_HALO + tm + SUBLANES, d), F32), pltpu.VMEM((tm, d), F32)],
        compiler_params=_params(("arbitrary", "arbitrary")),
        name="proj_conv",
    )(x2d, w_in, b_in, hgrn_lb, conv_w, conv_b, g, b)


def _conv_sample_kernel(u_ref, st_ref, cw_ref, cb_ref, g_ref, b_ref, h_ref, new_ref):
    t_new, sb, d = u_ref.shape
    nbuf = st_ref.shape[1]

    def ext_row(r, rows):
        if r < nbuf:
            return st_ref[rows, r, :]
        return u_ref[r - nbuf, rows, :]

    def body(gi, carry):
        rows = pl.ds(pl.multiple_of(gi * SUBLANES, SUBLANES), SUBLANES)
        for t in range(t_new):
            acc = jnp.zeros((SUBLANES, d), F32) + cb_ref[...]
            for j in range(CONV_WIDTH):
                acc = acc + cw_ref[j:j + 1, :] * ext_row(t + j, rows)
            h_ref[t, rows, :] = _silu(_layer_norm(acc, g_ref[...], b_ref[...])).astype(h_ref.dtype)
        return carry

    lax.fori_loop(0, sb // SUBLANES, body, 0)
    new_ref[:, 0:nbuf - t_new, :] = st_ref[:, t_new:nbuf, :]
    for t in range(t_new):
        new_ref[:, nbuf - t_new + t, :] = u_ref[t]


def _conv_sample(u, state, conv_w, conv_b, g, b):
    t, bsz, d = u.shape
    nbuf = state.shape[1]
    sb = SAMPLE_SEQ_BLOCK
    return pl.pallas_call(
        _conv_sample_kernel,
        grid=(bsz // sb,),
        in_specs=[pl.BlockSpec((t, sb, d), lambda i: (0, i, 0)),
                  pl.BlockSpec((sb, nbuf, d), lambda i: (i, 0, 0)),
                  _full(conv_w.shape), _full((1, d)), _full((1, d)), _full((1, d))],
        out_specs=[pl.BlockSpec((t, sb, d), lambda i: (0, i, 0)),
                   pl.BlockSpec((sb, nbuf, d), lambda i: (i, 0, 0))],
        out_shape=[jax.ShapeDtypeStruct((t, bsz, d), F32),
                   jax.ShapeDtypeStruct((bsz, nbuf, d), F32)],
        compiler_params=_params(("arbitrary",)),
        name="conv_sample",
    )(u, state, conv_w, conv_b, g, b)


def _hgrn_chunk(q, k, v, g, st, consts, chunk, sub):
    _, blk_mask, neg_masks = consts
    nb = chunk // sub
    lcum = g
    l_end = lcum[chunk - 1:chunk, :]

    o = _dot_nt((q * jnp.exp2(lcum)).astype(BF16), st.astype(BF16))

    if nb > 1:
        starts = [lcum[i * sub - 1:i * sub, :] for i in range(1, nb)]
        l_start = jnp.concatenate(
            [jnp.zeros((sub, HEAD_DIM), F32)]
            + [jnp.broadcast_to(s, (sub, HEAD_DIM)) for s in starts], axis=0)
        q_rel = (q * jnp.exp2(lcum - l_start)).astype(BF16)
        k_stack = jnp.concatenate(
            [k[0:i * sub] * jnp.exp2(starts[i - 1] - lcum[0:i * sub]) for i in range(1, nb)],
            axis=0).astype(BF16)
        v_stack = jnp.concatenate([v[0:i * sub] for i in range(1, nb)], axis=0).astype(BF16)
        scores = _dot_nt(q_rel, k_stack) * blk_mask
        o = o + _dot(scores.astype(BF16), v_stack)

    diag = []
    for i in range(nb):
        sl = slice(i * sub, (i + 1) * sub)
        qi, ki, li = q[sl], k[sl], lcum[sl]
        od = None
        for s in range(sub):
            dec = jnp.exp2(li - li[s:s + 1, :] + neg_masks[s])
            score = jnp.sum(qi * (ki[s:s + 1, :] * dec), axis=-1, keepdims=True)
            term = score * v[i * sub + s:i * sub + s + 1, :]
            od = term if od is None else od + term
        diag.append(od)
    o = o + (jnp.concatenate(diag, axis=0) if nb > 1 else diag[0])

    k_end = (k * jnp.exp2(l_end - lcum)).astype(BF16)
    st_new = st * jnp.exp2(l_end) + _dot(v.T.astype(BF16), k_end)
    return o, st_new


def _hgrn_chunk_mild(q, k, v, g, st, causal):
    chunk = q.shape[0]
    l_end = g[chunk - 1:chunk, :]
    q_dec = (q * jnp.exp2(g)).astype(BF16)
    k_inv = (k * jnp.exp2(-g)).astype(BF16)
    scores = jnp.where(causal, _dot_nt(q_dec, k_inv), 0.0).astype(BF16)
    o = _dot(jnp.concatenate([q_dec, scores], axis=1),
             jnp.concatenate([st.T.astype(BF16), v.astype(BF16)], axis=0))
    st_new = (st + _dot(v.T.astype(BF16), k_inv)) * jnp.exp2(l_end)
    return o, st_new


def _hgrn_consts(chunk, sub):
    nb = chunk // sub
    r = lax.broadcasted_iota(jnp.int32, (chunk, chunk), 0)
    c = lax.broadcasted_iota(jnp.int32, (chunk, chunk), 1)
    tri = jnp.where(c <= r, 1.0, 0.0).astype(BF16)
    row = lax.broadcasted_iota(jnp.int32, (sub, HEAD_DIM), 0)
    neg_masks = [jnp.where(row >= s, 0.0, -NEG_BIG).astype(F32) for s in range(sub)]
    blk_mask = None
    if nb > 1:
        width = sub * nb * (nb - 1) // 2
        rb = lax.broadcasted_iota(jnp.int32, (chunk, width), 0) // sub
        cc = lax.broadcasted_iota(jnp.int32, (chunk, width), 1)
        blk_mask = jnp.zeros((chunk, width), F32)
        off = 0
        for i in range(1, nb):
            hit = jnp.where(rb == i, jnp.where(cc >= off, jnp.where(cc < off + i * sub, 1.0, 0.0), 0.0), 0.0)
            blk_mask = blk_mask + hit
            off += i * sub
    return tri, blk_mask, neg_masks


def _cumsum_rows(tri_bf, g):
    g1 = g.astype(BF16)
    r1 = g - g1.astype(F32)
    g2 = r1.astype(BF16)
    g3 = (r1 - g2.astype(F32)).astype(BF16)
    return _dot(tri_bf, g1) + _dot(tri_bf, g2) + _dot(tri_bf, g3)


def _head_norm(o, gn):
    return o * lax.rsqrt(jnp.mean(o * o, axis=-1, keepdims=True) + LN_EPS) * gn


def _hgrn_prompt_kernel(q_ref, k_ref, v_ref, lf_ref, sg_ref, gn_ref, s0_ref, o_ref, s_ref,
                        st_ref, lc_ref, mild_ref, *, chunk, sub):
    tb = q_ref.shape[1]
    n_chunks = tb // chunk
    heads = q_ref.shape[2] // HEAD_DIM
    j = pl.program_id(2)
    consts = _hgrn_consts(chunk, sub)
    gn = gn_ref[...]

    @pl.when(j == 0)
    def _():
        for h in range(heads):
            st_ref[h] = s0_ref[0, h].T

    causal = (lax.broadcasted_iota(jnp.int32, (chunk, chunk), 1)
              <= lax.broadcasted_iota(jnp.int32, (chunk, chunk), 0))

    def prepare(n, slot):
        rows = pl.ds(pl.multiple_of(n * chunk, chunk), chunk)
        lcum = _cumsum_rows(consts[0], lf_ref[0, rows, :]) * LOG2_E
        lc_ref[slot] = lcum
        mild_ref[slot] = (jnp.min(lcum[chunk - 1:chunk, :]) >= -HGRN_MILD_LOG2).astype(jnp.int32)

    prepare(0, 0)

    def body(n, carry):
        rows = pl.ds(pl.multiple_of(n * chunk, chunk), chunk)
        slot = lax.rem(n, 2)
        mild = mild_ref[slot] == 1

        def run(step):
            for h in range(heads):
                lanes = slice(h * HEAD_DIM, (h + 1) * HEAD_DIM)
                o, st = step(q_ref[0, rows, lanes].astype(F32), k_ref[0, rows, lanes].astype(F32),
                             v_ref[0, rows, lanes].astype(F32), lc_ref[slot, :, lanes], st_ref[h])
                st_ref[h] = st
                o_ref[0, rows, lanes] = (_head_norm(o, gn) * sg_ref[0, rows, lanes].astype(F32)).astype(o_ref.dtype)
            prepare(jnp.minimum(n + 1, n_chunks - 1), 1 - slot)

        @pl.when(mild)
        def _():
            run(lambda q, k, v, g, st: _hgrn_chunk_mild(q, k, v, g, st, causal))

        @pl.when(jnp.logical_not(mild))
        def _():
            run(lambda q, k, v, g, st: _hgrn_chunk(q, k, v, g, st, consts, chunk, sub))

        return carry

    lax.fori_loop(0, n_chunks, body, 0)

    @pl.when(j == pl.num_programs(2) - 1)
    def _():
        for h in range(heads):
            s_ref[0, h] = st_ref[h].T


def _hgrn_prompt(q, k, v, lf, sg, gnorm, s0):
    bsz, t, d = q.shape
    tb = HGRN_BLOCK
    hg = HGRN_HEAD_GROUP
    blk = pl.BlockSpec((1, tb, hg * HEAD_DIM), lambda b, g, j: (b, j, g))
    sblk = pl.BlockSpec((1, hg, HEAD_DIM, HEAD_DIM), lambda b, g, j: (b, g, 0, 0))
    return pl.pallas_call(
        functools.partial(_hgrn_prompt_kernel, chunk=HGRN_CHUNK, sub=HGRN_SUB),
        grid=(bsz, HGRN_HEADS // hg, t // tb),
        in_specs=[blk, blk, blk, blk, blk, _full((1, HEAD_DIM)), sblk],
        out_specs=[blk, sblk],
        out_shape=[jax.ShapeDtypeStruct((bsz, t, d), BF16),
                   jax.ShapeDtypeStruct(s0.shape, F32)],
        scratch_shapes=[pltpu.VMEM((hg, HEAD_DIM, HEAD_DIM), F32),
                        pltpu.VMEM((2, HGRN_CHUNK, hg * HEAD_DIM), F32), pltpu.SMEM((2,), jnp.int32)],
        compiler_params=_params(("arbitrary", "arbitrary", "arbitrary")),
        name="hgrn_prompt",
    )(q, k, v, lf, sg, gnorm, s0)


def _hgrn_sample_kernel(q_ref, k_ref, v_ref, lf_ref, sg_ref, gn_ref, s0_ref, o_ref, s_ref):
    sb, t_new, d_model = q_ref.shape
    gn = gn_ref[...]
    pad_rows = SAMPLE_PAD - t_new - 1

    def body(b, carry):
        q, k, v, sg = q_ref[b], k_ref[b], v_ref[b], sg_ref[b]
        lf = lf_ref[b]
        lrow = []
        for t in range(t_new):
            lrow.append(lf[t:t + 1, :] if t == 0 else lrow[-1] + lf[t:t + 1, :])
        lcum = jnp.concatenate(lrow, axis=0)
        l_end = lrow[-1]
        qe = q * jnp.exp(lcum)
        ke = k * jnp.exp(l_end - lcum)
        f_end = jnp.exp(l_end)
        zeros = jnp.zeros((pad_rows, d_model), F32)
        zero_row = jnp.zeros((1, d_model), F32)
        qe_p = jnp.concatenate([qe, zero_row, zeros], axis=0)
        ke_p = jnp.concatenate([ke, f_end, zeros], axis=0)
        v_p = jnp.concatenate([v, zero_row, zeros], axis=0)
        for h in range(HGRN_HEADS):
            lanes = slice(h * HEAD_DIM, (h + 1) * HEAD_DIM)
            s0 = s0_ref[b, h]
            o = _dot(qe_p[:, lanes].astype(BF16), s0.astype(BF16))[0:t_new]
            rows = []
            for t in range(t_new):
                ot = None
                for s in range(t + 1):
                    prod = q[t:t + 1, lanes] * k[s:s + 1, lanes]
                    if s < t:
                        prod = prod * jnp.exp(lrow[t][:, lanes] - lrow[s][:, lanes])
                    term = jnp.sum(prod, axis=-1, keepdims=True) * v[s:s + 1, lanes]
                    ot = term if ot is None else ot + term
                rows.append(ot)
            o = o + jnp.concatenate(rows, axis=0)
            ke_t = ke_p[:, lanes].T
            decay = ke_t[:, t_new:t_new + 1]
            s_ref[b, h] = s0 * decay + _dot(ke_t.astype(BF16), v_p[:, lanes].astype(BF16))
            o_ref[b, :, lanes] = _head_norm(o, gn) * sg[:, lanes]
        return carry

    lax.fori_loop(0, sb, body, 0)


def _hgrn_sample(q, k, v, lf, sg, gnorm, s0):
    bsz, t, d = q.shape
    sb = HGRN_SAMPLE_BLOCK
    blk = pl.BlockSpec((sb, t, d), lambda g: (g, 0, 0))
    sblk = pl.BlockSpec((sb, HGRN_HEADS, HEAD_DIM, HEAD_DIM), lambda g: (g, 0, 0, 0))
    return pl.pallas_call(
        _hgrn_sample_kernel,
        grid=(bsz // sb,),
        in_specs=[blk, blk, blk, blk, blk, _full((1, HEAD_DIM)), sblk],
        out_specs=[blk, sblk],
        out_shape=[jax.ShapeDtypeStruct((bsz, t, d), F32),
                   jax.ShapeDtypeStruct(s0.shape, F32)],
        compiler_params=_params(("arbitrary",)),
        name="hgrn_sample",
    )(q, k, v, lf, sg, gnorm, s0)


def _merge_kernel(x_ref, ha_ref, hb_ref, ga_ref, gb_ref, wa_ref, wb_ref, wo_ref, g_ref, b_ref, o_ref):
    ya = _dot(ha_ref[...].astype(BF16), wa_ref[...])
    yb = _dot(hb_ref[...].astype(BF16), wb_ref[...])
    mixed = ga_ref[...].astype(F32) * ya + gb_ref[...].astype(F32) * yb
    z = DEEPNORM_ALPHA * x_ref[...] + _dot(mixed.astype(BF16), wo_ref[...])
    o_ref[...] = _layer_norm(z, g_ref[...], b_ref[...])


def _merge(x2d, ha, hb, ga, gb, wa, wb, wo, g, b):
    n, d = x2d.shape
    tm = MERGE_TOKEN_BLOCK if n % MERGE_TOKEN_BLOCK == 0 else TOKEN_BLOCK
    row = pl.BlockSpec((tm, d), lambda i: (i, 0))
    return pl.pallas_call(
        _merge_kernel,
        grid=(n // tm,),
        in_specs=[row] * 5 + [_full((d, d))] * 3 + [_full((1, d))] * 2,
        out_specs=row,
        out_shape=jax.ShapeDtypeStruct((n, d), F32),
        compiler_params=_params(("arbitrary",)),
        name="merge",
    )(x2d, ha, hb, ga, gb, wa, wb, wo, g, b)


def _route_t(s, bias):
    n = s.shape[1]
    neg = -jnp.inf
    sb = (s + bias).reshape(N_GROUPS, GROUP_SIZE, n)
    s3 = s.reshape(N_GROUPS, GROUP_SIZE, n)
    e_in_g = lax.broadcasted_iota(jnp.int32, sb.shape, 1)
    m1 = jnp.max(sb, axis=1, keepdims=True)
    first = jnp.min(jnp.where(sb == m1, e_in_g, GROUP_SIZE), axis=1, keepdims=True)
    m2 = jnp.max(jnp.where(e_in_g == first, neg, sb), axis=1, keepdims=True)
    gscore = (m1 + m2)[:, 0, :]
    gid = lax.broadcasted_iota(jnp.int32, gscore.shape, 0)
    gsel = jnp.zeros(gscore.shape, F32)
    for _ in range(TOPK_GROUPS):
        gm = jnp.max(gscore, axis=0, keepdims=True)
        pick = jnp.min(jnp.where(gscore == gm, gid, N_GROUPS), axis=0, keepdims=True)
        hit = gid == pick
        gsel = jnp.where(hit, 1.0, gsel)
        gscore = jnp.where(hit, neg, gscore)
    cand = jnp.where(gsel[:, None, :] > 0.5, sb, neg)
    eid = lax.broadcasted_iota(jnp.int32, sb.shape, 0) * GROUP_SIZE + e_in_g
    esel = jnp.zeros(sb.shape, F32)
    for _ in range(TOP_K):
        em = jnp.max(jnp.max(cand, axis=1, keepdims=True), axis=0, keepdims=True)
        masked = jnp.where(cand == em, eid, N_EXPERTS)
        pick = jnp.min(jnp.min(masked, axis=1, keepdims=True), axis=0, keepdims=True)
        hit = eid == pick
        esel = jnp.where(hit, 1.0, esel)
        cand = jnp.where(hit, neg, cand)
    w = esel * s3
    tot = jnp.sum(jnp.sum(w, axis=1, keepdims=True), axis=0, keepdims=True)
    return (w / tot * ROUTED_SCALE).reshape(N_EXPERTS, n), jnp.max(esel, axis=1)


def _split_bf16(a):
    hi = a.astype(BF16)
    return hi, (a - hi.astype(F32)).astype(BF16)


def _moe_kernel(x_ref, p_ref, wrt_ref, rb_ref, wg_ref, wu_ref, wd_ref, sg_ref, su_ref, sd_ref,
                g2_ref, b2_ref, pg_ref, pp_ref, o_ref,
                xb_ref, gt_ref, acc_ref, key_ref, ok_ref, xg_ref, gg_ref, pt_ref, accg_ref):
    j = pl.program_id(1)
    eb = wg_ref.shape[0]
    tm = x_ref.shape[0]
    cap = xg_ref.shape[0]
    steps_per_group = GROUP_SIZE // eb
    group = j // steps_per_group
    step_in_group = j - group * steps_per_group

    @pl.when(j == 0)
    def _():
        x = x_ref[...]
        logits = lax.dot_general(wrt_ref[...], x, (((1,), (1,)), ((), ())),
                                 precision=lax.Precision.HIGHEST, preferred_element_type=F32)
        gates, gsel = _route_t(_sigmoid(logits), rb_ref[...])
        gt_ref[...] = gates
        earlier = (lax.broadcasted_iota(jnp.int32, (tm, tm), 0)
                   < lax.broadcasted_iota(jnp.int32, (tm, tm), 1))
        rank = _dot(gsel.astype(BF16), jnp.where(earlier, 1.0, 0.0).astype(BF16))
        key = jnp.where(gsel > 0.5, rank, -1.0)
        for g in range(N_GROUPS):
            key_ref[g] = key[g:g + 1, :]
        xb = x.astype(BF16)
        xb_ref[...] = xb
        hs = _silu(_dot(xb, sg_ref[...])) * _dot(xb, su_ref[...])
        acc_ref[...] = _dot(hs.astype(BF16), sd_ref[...])

    slab = pl.multiple_of(group * GROUP_SIZE, GROUP_SIZE)

    @pl.when(step_in_group == 0)
    def _():
        key_row = key_ref[group]
        count = jnp.sum(jnp.where(key_row >= 0.0, 1.0, 0.0))
        fits = count <= float(cap)
        ok_ref[0] = fits.astype(jnp.int32)

        @pl.when(fits)
        def _():
            slot = lax.broadcasted_iota(jnp.int32, (cap, tm), 0).astype(F32)
            pick = jnp.where(slot == key_row, 1.0, 0.0).astype(BF16)
            xg_ref[...] = _dot(pick, xb_ref[...]).astype(BF16)
            g_rows = jnp.concatenate([gt_ref[pl.ds(slab, GROUP_SIZE), :],
                                      jnp.zeros((LANES - GROUP_SIZE, tm), F32)], axis=0)
            g_hi, g_lo = _split_bf16(g_rows.T)
            gg_ref[...] = _dot(pick, g_hi) + _dot(pick, g_lo)
            key_col = jnp.broadcast_to(key_row, (SUBLANES, tm)).T[:, 0:1]
            slot_t = lax.broadcasted_iota(jnp.int32, (tm, cap), 1).astype(F32)
            pt_ref[...] = jnp.where(slot_t == key_col, 1.0, 0.0).astype(BF16)
            accg_ref[...] = jnp.zeros(accg_ref.shape, F32)

    compact = ok_ref[0] == 1
    in_group = step_in_group * eb

    def experts(rows, gate_cols, sink_ref):
        for r in range(eb):
            gate = gate_cols[:, r:r + 1]
            for off in range(eb, GROUP_SIZE, eb):
                gate = jnp.where(in_group == off, gate_cols[:, off + r:off + r + 1], gate)
            h = _silu(_dot(rows, wg_ref[r])) * _dot(rows, wu_ref[r])
            sink_ref[...] += _dot((h * gate).astype(BF16), wd_ref[r])

    @pl.when(compact)
    def _():
        experts(xg_ref[...], gg_ref[...], accg_ref)

    @pl.when(jnp.logical_not(compact))
    def _():
        experts(xb_ref[...], gt_ref[pl.ds(slab, GROUP_SIZE), :].T, acc_ref)

    @pl.when(jnp.logical_and(compact, step_in_group == steps_per_group - 1))
    def _():
        acc_ref[...] += _dot(pt_ref[...], accg_ref[...].astype(BF16))

    @pl.when(j == pl.num_programs(1) - 1)
    def _():
        x2 = _layer_norm(DEEPNORM_ALPHA * x_ref[...] + acc_ref[...], g2_ref[...], b2_ref[...])
        gate = _sigmoid(_dot(x2.astype(BF16), pg_ref[...]))
        o_ref[...] = x2 + gate * _dot(p_ref[...].astype(BF16), pp_ref[...])


def _moe(x2d, p2d, wrt, rbias, wg, wu, wd, sg, su, sd, g2, b2, pg, pp):
    n, d = x2d.shape
    tm = MOE_TOKEN_BLOCK
    eb = EXPERTS_PER_STEP
    cap = MOE_GROUP_CAP
    assert GROUP_SIZE % eb == 0 and n % tm == 0
    dp = p2d.shape[1]
    ff = wg.shape[2]
    sff = sg.shape[1]
    row = pl.BlockSpec((tm, d), lambda i, j: (i, 0))
    return pl.pallas_call(
        _moe_kernel,
        grid=(n // tm, N_EXPERTS // eb),
        in_specs=[row, pl.BlockSpec((tm, dp), lambda i, j: (i, 0)),
                  _full((N_EXPERTS, d)), _full((N_EXPERTS, 1)),
                  pl.BlockSpec((eb, d, ff), lambda i, j: (j, 0, 0)),
                  pl.BlockSpec((eb, d, ff), lambda i, j: (j, 0, 0)),
                  pl.BlockSpec((eb, ff, d), lambda i, j: (j, 0, 0)),
                  _full((d, sff)), _full((d, sff)), _full((sff, d)),
                  _full((1, d)), _full((1, d)), _full((d, d)), _full((dp, d))],
        out_specs=row,
        out_shape=jax.ShapeDtypeStruct((n, d), F32),
        scratch_shapes=[pltpu.VMEM((tm, d), BF16), pltpu.VMEM((N_EXPERTS, tm), F32),
                        pltpu.VMEM((tm, d), F32),
                        pltpu.VMEM((N_GROUPS, 1, tm), F32), pltpu.SMEM((1,), jnp.int32),
                        pltpu.VMEM((cap, d), BF16), pltpu.VMEM((cap, LANES), F32),
                        pltpu.VMEM((tm, cap), BF16), pltpu.VMEM((cap, d), F32)],
        compiler_params=_params(("arbitrary", "arbitrary")),
        name="moe",
    )(x2d, p2d, wrt, rbias, wg, wu, wd, sg, su, sd, g2, b2, pg, pp)


def kernel(x_prompt, x_sample, p_prompt, p_sample, state_conv, state_hgrn, w_in, b_in, hgrn_lb, conv_w, conv_b, conv_ln_g, conv_ln_b, w_conv_out, hgrn_norm_g, w_hgrn_out, w_o, ln1_g, ln1_b, w_router, router_bias, w_exp_gate, w_exp_up, w_exp_down, w_sh_gate, w_sh_up, w_sh_down, ln2_g, ln2_b, w_ple_gate, w_ple_proj):
    assert w_in.shape[0] == DEPTH == 1
    bp, tp, d = x_prompt.shape
    bs, ts, _ = x_sample.shape
    nbuf = CONV_WIDTH - 1
    i = 0
    layer = lambda a: a.reshape(a.shape[1:])
    row = lambda a: a.reshape(1, -1)
    bf = lambda a: layer(a).astype(BF16)

    w_in_b, wco, who, wo = bf(w_in), bf(w_conv_out), bf(w_hgrn_out), bf(w_o)
    weg, weu, wed = bf(w_exp_gate), bf(w_exp_up), bf(w_exp_down)
    wsg, wsu, wsd = bf(w_sh_gate), bf(w_sh_up), bf(w_sh_down)
    wpg, wpp = bf(w_ple_gate), bf(w_ple_proj)
    wrt = layer(w_router).T
    rbias = router_bias.reshape(N_EXPERTS, 1)
    gnorm = row(hgrn_norm_g)

    def tail(x2d, p2d, ha, hb, ga, gb):
        x1 = _merge(x2d, ha, hb, ga, gb, wco, who, wo, row(ln1_g), row(ln1_b))
        return _moe(x1, p2d, wrt, rbias, weg, weu, wed, wsg, wsu, wsd, row(ln2_g), row(ln2_b), wpg, wpp)

    xp = x_prompt.reshape(bp * tp, d)
    ha, u_tail, q, k, v, lf, sg, ga, gb = _proj_conv(xp, tp, w_in_b, row(b_in), hgrn_lb, i, layer(conv_w),
                                                     row(conv_b), row(conv_ln_g), row(conv_ln_b))
    seq = lambda a: a.reshape(bp, tp, d)
    hb, hgrn_p = _hgrn_prompt(seq(q), seq(k), seq(v), seq(lf), seq(sg), gnorm,
                              jnp.zeros((bp, HGRN_HEADS, HEAD_DIM, HEAD_DIM), F32))
    y_p = tail(xp, p_prompt.reshape(bp * tp, -1), ha, hb.reshape(bp * tp, d), ga, gb)
    conv_p = u_tail.reshape(bp, CONV_HALO, d)[:, CONV_HALO - nbuf:, :]

    tmaj = lambda a: jnp.swapaxes(a, 0, 1).reshape(ts * bs, -1)
    xs = tmaj(x_sample)
    u, q, k, v, lf, sg, ga, gb = _proj(xs, w_in_b, row(b_in), hgrn_lb, i)
    seq = lambda a: a.reshape(ts, bs, d)
    ha, conv_s = _conv_sample(seq(u), layer(state_conv), layer(conv_w), row(conv_b), row(conv_ln_g),
                              row(conv_ln_b))
    bmaj = lambda a: jnp.swapaxes(seq(a), 0, 1).astype(F32)
    hb, hgrn_s = _hgrn_sample(bmaj(q), bmaj(k), bmaj(v), bmaj(lf), bmaj(sg), gnorm, layer(state_hgrn))
    y_s = tail(xs, tmaj(layer(p_sample)), ha.reshape(ts * bs, d), tmaj(hb), ga, gb)
    y_s = jnp.swapaxes(y_s.reshape(ts, bs, d), 0, 1)

    return (y_p.reshape(bp, tp, d), y_s, conv_p[None], hgrn_p[None], conv_s[None], hgrn_s[None])
```

```python
import functools

import jax
import jax.numpy as jnp
from jax import lax
from jax.experimental import pallas as pl
from jax.experimental.pallas import tpu as pltpu

F32 = jnp.float32
BF16 = jnp.bfloat16

D_MODEL = 1024
CONV_WIDTH = 31
HGRN_HEADS = 8
HEAD_DIM = D_MODEL // HGRN_HEADS
N_EXPERTS = 64
N_GROUPS = 8
GROUP_SIZE = N_EXPERTS // N_GROUPS
TOPK_GROUPS = 4
TOP_K = 8
EXPERT_FF = 256
ROUTED_SCALE = 2.5
LN_EPS = 1e-5
LOG2_E = 1.4426950408889634
NEG_BIG = 1e30
SUBLANES = 8
LANES = 128
DEPTH = 1
DEEPNORM_ALPHA = (2.0 * DEPTH) ** 0.25

V7X_VMEM_BYTES = 64 * 1024 * 1024
VMEM_LIMIT = V7X_VMEM_BYTES - 8 * 1024 * 1024

TOKEN_BLOCK = 512
CONV_ROWS = 128
NORM_ROWS = 64
SAMPLE_SEQ_BLOCK = 32
CONV_HALO = 32
HGRN_BLOCK = 1024
HGRN_HEAD_GROUP = 8
HGRN_CHUNK = 64
HGRN_MILD_LOG2 = 96.0
HGRN_SUB = 8
HGRN_SAMPLE_BLOCK = 8
SAMPLE_PAD = 16
MOE_TOKEN_BLOCK = 1024
MOE_SUB_BLOCK = 512
MOE_GROUP_CAP = 288
MERGE_TOKEN_BLOCK = 1024
EXPERTS_PER_STEP = 4


def _sigmoid(x):
    return 1.0 / (1.0 + jnp.exp(-x))


def _silu(x):
    return x * _sigmoid(x)


def _layer_norm(x, g, b):
    mu = jnp.mean(x, axis=-1, keepdims=True)
    xc = x - mu
    var = jnp.mean(xc * xc, axis=-1, keepdims=True)
    return xc * lax.rsqrt(var + LN_EPS) * g + b


def _dot(a, b):
    return jnp.dot(a, b, preferred_element_type=F32)


def _dot_nt(a, b):
    return lax.dot_general(a, b, (((1,), (1,)), ((), ())), preferred_element_type=F32)


def _full(shape):
    return pl.BlockSpec(shape, lambda *_: (0,) * len(shape))


def _params(sem):
    return pltpu.CompilerParams(dimension_semantics=sem, vmem_limit_bytes=VMEM_LIMIT)


def _proj_columns(x_ref, w_ref, b_ref, lbp_ref, layer, emit_u, q_ref, k_ref, v_ref, lf_ref, sg_ref, ga_ref, gb_ref):
    D = D_MODEL
    xb = x_ref[...].astype(BF16)

    def col(j):
        return _dot(xb, w_ref[:, j * D:(j + 1) * D]) + b_ref[:, j * D:(j + 1) * D]

    emit_u(col(0) * _sigmoid(col(1)))
    hl = lbp_ref[...]
    e = jnp.exp(hl - jnp.max(hl, axis=0, keepdims=True))
    lb = jnp.sum(e[:layer + 1], axis=0, keepdims=True) / jnp.sum(e, axis=0, keepdims=True)
    fz = col(2)
    lf_ref[...] = jnp.log(lb + (1.0 - lb) * _sigmoid(fz))
    k_ref[...] = ((1.0 - lb) * _sigmoid(-fz)).astype(k_ref.dtype)
    v_ref[...] = col(3).astype(v_ref.dtype)
    q_ref[...] = _silu(col(4)).astype(q_ref.dtype)
    sg_ref[...] = _silu(col(5)).astype(sg_ref.dtype)
    ga_ref[...] = _sigmoid(col(6)).astype(ga_ref.dtype)
    gb_ref[...] = _sigmoid(col(7)).astype(gb_ref.dtype)


def _proj_kernel(x_ref, w_ref, b_ref, lbp_ref, u_ref, q_ref, k_ref, v_ref, lf_ref, sg_ref,
                 ga_ref, gb_ref, *, layer):
    def emit_u(u):
        u_ref[...] = u

    _proj_columns(x_ref, w_ref, b_ref, lbp_ref, layer, emit_u, q_ref, k_ref, v_ref, lf_ref, sg_ref, ga_ref, gb_ref)


def _proj(x2d, w_in, b_in, hgrn_lb, layer):
    n, d = x2d.shape
    tm = TOKEN_BLOCK
    cols = w_in.shape[1]
    row = pl.BlockSpec((tm, d), lambda i: (i, 0))
    outs = [jax.ShapeDtypeStruct((n, d), dt) for dt in (F32, BF16, BF16, BF16, F32, BF16, BF16, BF16)]
    return pl.pallas_call(
        functools.partial(_proj_kernel, layer=layer),
        grid=(n // tm,),
        in_specs=[row,
                  pl.BlockSpec((d, cols), lambda i: (0, 0), pipeline_mode=pl.Buffered(1)),
                  _full((1, cols)), _full(hgrn_lb.shape)],
        out_specs=[row] * 8,
        out_shape=outs,
        compiler_params=_params(("arbitrary",)),
        name="proj",
    )(x2d, w_in, b_in, hgrn_lb)


def _conv_tap_groups():
    shift = CONV_HALO - (CONV_WIDTH - 1)
    groups = [[] for _ in range(SUBLANES)]
    for j in range(CONV_WIDTH):
        groups[(j + shift) % SUBLANES].append((j, (j + shift) // SUBLANES))
    return groups


def _conv_rows(ext_ref, cw_ref, cb_ref, y_ref, base, rows):
    for l in range(D_MODEL // LANES):
        lanes = slice(l * LANES, (l + 1) * LANES)
        acc = None
        for res, taps in enumerate(_conv_tap_groups()):
            part = None
            for j, a in taps:
                term = cw_ref[j:j + 1, lanes] * ext_ref[pl.ds(base + SUBLANES * a, rows + SUBLANES), lanes]
                part = term if part is None else part + term
            part = part[res:res + rows, :]
            acc = part if acc is None else acc + part
        y_ref[pl.ds(base, rows), lanes] = acc + cb_ref[:, lanes]


def _proj_conv_kernel(x_ref, w_ref, b_ref, lbp_ref, cw_ref, cb_ref, g_ref, bb_ref,
                      h_ref, tail_ref, q_ref, k_ref, v_ref, lf_ref, sg_ref, ga_ref, gb_ref,
                      ext_ref, y_ref, *, layer):
    tm = x_ref.shape[0]

    @pl.when(pl.program_id(1) == 0)
    def _():
        ext_ref[0:CONV_HALO, :] = jnp.zeros((CONV_HALO, D_MODEL), F32)
        ext_ref[CONV_HALO + tm:CONV_HALO + tm + SUBLANES, :] = jnp.zeros((SUBLANES, D_MODEL), F32)

    def emit_u(u):
        ext_ref[CONV_HALO:CONV_HALO + tm, :] = u
        tail_ref[...] = u[tm - CONV_HALO:tm, :]
        for r in range(tm // CONV_ROWS):
            _conv_rows(ext_ref, cw_ref, cb_ref, y_ref, r * CONV_ROWS, CONV_ROWS)
        for r in range(tm // NORM_ROWS):
            rows = slice(r * NORM_ROWS, (r + 1) * NORM_ROWS)
            h_ref[rows, :] = _silu(_layer_norm(y_ref[rows, :], g_ref[...], bb_ref[...])).astype(h_ref.dtype)
        ext_ref[0:CONV_HALO, :] = ext_ref[tm:tm + CONV_HALO, :]

    _proj_columns(x_ref, w_ref, b_ref, lbp_ref, layer, emit_u, q_ref, k_ref, v_ref, lf_ref, sg_ref, ga_ref, gb_ref)


def _proj_conv(x2d, seq_len, w_in, b_in, hgrn_lb, layer, conv_w, conv_b, g, b):
    n, d = x2d.shape
    tm = TOKEN_BLOCK
    nt = seq_len // tm
    cols = w_in.shape[1]
    row = pl.BlockSpec((tm, d), lambda s, j: (s * nt + j, 0))
    outs = [jax.ShapeDtypeStruct((n, d), BF16), jax.ShapeDtypeStruct((n // seq_len * CONV_HALO, d), F32)]
    outs += [jax.ShapeDtypeStruct((n, d), dt) for dt in (BF16, BF16, BF16, F32, BF16, BF16, BF16)]
    return pl.pallas_call(
        functools.partial(_proj_conv_kernel, layer=layer),
        grid=(n // seq_len, nt),
        in_specs=[row,
                  pl.BlockSpec((d, cols), lambda s, j: (0, 0), pipeline_mode=pl.Buffered(1)),
                  _full((1, cols)), _full(hgrn_lb.shape),
                  _full(conv_w.shape), _full((1, d)), _full((1, d)), _full((1, d))],
        out_specs=[row, pl.BlockSpec((CONV_HALO, d), lambda s, j: (s, 0))] + [row] * 7,
        out_shape=outs,
        scratch_shapes=[pltpu.VMEM((CONV_HALO + tm + SUBLANES, d), F32), pltpu.VMEM((tm, d), F32)],
        compiler_params=_params(("arbitrary", "arbitrary")),
        name="proj_conv",
    )(x2d, w_in, b_in, hgrn_lb, conv_w, conv_b, g, b)


def _conv_sample_kernel(u_ref, st_ref, cw_ref, cb_ref, g_ref, b_ref, h_ref, new_ref):
    t_new, sb, d = u_ref.shape
    nbuf = st_ref.shape[1]

    def ext_row(r, rows):
        if r < nbuf:
            return st_ref[rows, r, :]
        return u_ref[r - nbuf, rows, :]

    def body(gi, carry):
        rows = pl.ds(pl.multiple_of(gi * SUBLANES, SUBLANES), SUBLANES)
        for t in range(t_new):
            acc = jnp.zeros((SUBLANES, d), F32) + cb_ref[...]
            for j in range(CONV_WIDTH):
                acc = acc + cw_ref[j:j + 1, :] * ext_row(t + j, rows)
            h_ref[t, rows, :] = _silu(_layer_norm(acc, g_ref[...], b_ref[...])).astype(h_ref.dtype)
        return carry

    lax.fori_loop(0, sb // SUBLANES, body, 0)
    new_ref[:, 0:nbuf - t_new, :] = st_ref[:, t_new:nbuf, :]
    for t in range(t_new):
        new_ref[:, nbuf - t_new + t, :] = u_ref[t]


def _conv_sample(u, state, conv_w, conv_b, g, b):
    t, bsz, d = u.shape
    nbuf = state.shape[1]
    sb = SAMPLE_SEQ_BLOCK
    return pl.pallas_call(
        _conv_sample_kernel,
        grid=(bsz // sb,),
        in_specs=[pl.BlockSpec((t, sb, d), lambda i: (0, i, 0)),
                  pl.BlockSpec((sb, nbuf, d), lambda i: (i, 0, 0)),
                  _full(conv_w.shape), _full((1, d)), _full((1, d)), _full((1, d))],
        out_specs=[pl.BlockSpec((t, sb, d), lambda i: (0, i, 0)),
                   pl.BlockSpec((sb, nbuf, d), lambda i: (i, 0, 0))],
        out_shape=[jax.ShapeDtypeStruct((t, bsz, d), F32),
                   jax.ShapeDtypeStruct((bsz, nbuf, d), F32)],
        compiler_params=_params(("arbitrary",)),
        name="conv_sample",
    )(u, state, conv_w, conv_b, g, b)


def _hgrn_chunk(q, k, v, g, st, consts, chunk, sub):
    _, blk_mask, neg_masks = consts
    nb = chunk // sub
    lcum = g
    l_end = lcum[chunk - 1:chunk, :]

    o = _dot_nt((q * jnp.exp2(lcum)).astype(BF16), st.astype(BF16))

    if nb > 1:
        starts = [lcum[i * sub - 1:i * sub, :] for i in range(1, nb)]
        l_start = jnp.concatenate(
            [jnp.zeros((sub, HEAD_DIM), F32)]
            + [jnp.broadcast_to(s, (sub, HEAD_DIM)) for s in starts], axis=0)
        q_rel = (q * jnp.exp2(lcum - l_start)).astype(BF16)
        k_stack = jnp.concatenate(
            [k[0:i * sub] * jnp.exp2(starts[i - 1] - lcum[0:i * sub]) for i in range(1, nb)],
            axis=0).astype(BF16)
        v_stack = jnp.concatenate([v[0:i * sub] for i in range(1, nb)], axis=0).astype(BF16)
        scores = _dot_nt(q_rel, k_stack) * blk_mask
        o = o + _dot(scores.astype(BF16), v_stack)

    diag = []
    for i in range(nb):
        sl = slice(i * sub, (i + 1) * sub)
        qi, ki, li = q[sl], k[sl], lcum[sl]
        od = None
        for s in range(sub):
            dec = jnp.exp2(li - li[s:s + 1, :] + neg_masks[s])
            score = jnp.sum(qi * (ki[s:s + 1, :] * dec), axis=-1, keepdims=True)
            term = score * v[i * sub + s:i * sub + s + 1, :]
            od = term if od is None else od + term
        diag.append(od)
    o = o + (jnp.concatenate(diag, axis=0) if nb > 1 else diag[0])

    k_end = (k * jnp.exp2(l_end - lcum)).astype(BF16)
    st_new = st * jnp.exp2(l_end) + _dot(v.T.astype(BF16), k_end)
    return o, st_new


def _hgrn_chunk_mild(q, k, v, g, st, causal):
    chunk = q.shape[0]
    l_end = g[chunk - 1:chunk, :]
    q_dec = (q * jnp.exp2(g)).astype(BF16)
    k_inv = (k * jnp.exp2(-g)).astype(BF16)
    scores = jnp.where(causal, _dot_nt(q_dec, k_inv), 0.0).astype(BF16)
    o = _dot(jnp.concatenate([q_dec, scores], axis=1),
             jnp.concatenate([st.T.astype(BF16), v.astype(BF16)], axis=0))
    st_new = (st + _dot(v.T.astype(BF16), k_inv)) * jnp.exp2(l_end)
    return o, st_new


def _hgrn_consts(chunk, sub):
    nb = chunk // sub
    r = lax.broadcasted_iota(jnp.int32, (chunk, chunk), 0)
    c = lax.broadcasted_iota(jnp.int32, (chunk, chunk), 1)
    tri = jnp.where(c <= r, 1.0, 0.0).astype(BF16)
    row = lax.broadcasted_iota(jnp.int32, (sub, HEAD_DIM), 0)
    neg_masks = [jnp.where(row >= s, 0.0, -NEG_BIG).astype(F32) for s in range(sub)]
    blk_mask = None
    if nb > 1:
        width = sub * nb * (nb - 1) // 2
        rb = lax.broadcasted_iota(jnp.int32, (chunk, width), 0) // sub
        cc = lax.broadcasted_iota(jnp.int32, (chunk, width), 1)
        blk_mask = jnp.zeros((chunk, width), F32)
        off = 0
        for i in range(1, nb):
            hit = jnp.where(rb == i, jnp.where(cc >= off, jnp.where(cc < off + i * sub, 1.0, 0.0), 0.0), 0.0)
            blk_mask = blk_mask + hit
            off += i * sub
    return tri, blk_mask, neg_masks


def _cumsum_rows(tri_bf, g):
    g1 = g.astype(BF16)
    r1 = g - g1.astype(F32)
    g2 = r1.astype(BF16)
    g3 = (r1 - g2.astype(F32)).astype(BF16)
    return _dot(tri_bf, g1) + _dot(tri_bf, g2) + _dot(tri_bf, g3)


def _head_norm(o, gn):
    return o * lax.rsqrt(jnp.mean(o * o, axis=-1, keepdims=True) + LN_EPS) * gn


def _hgrn_prompt_kernel(q_ref, k_ref, v_ref, lf_ref, sg_ref, gn_ref, s0_ref, o_ref, s_ref,
                        st_ref, lc_ref, mild_ref, *, chunk, sub):
    tb = q_ref.shape[1]
    n_chunks = tb // chunk
    heads = q_ref.shape[2] // HEAD_DIM
    j = pl.program_id(2)
    consts = _hgrn_consts(chunk, sub)
    gn = gn_ref[...]

    @pl.when(j == 0)
    def _():
        for h in range(heads):
            st_ref[h] = s0_ref[0, h].T

    causal = (lax.broadcasted_iota(jnp.int32, (chunk, chunk), 1)
              <= lax.broadcasted_iota(jnp.int32, (chunk, chunk), 0))

    def prepare(n, slot):
        rows = pl.ds(pl.multiple_of(n * chunk, chunk), chunk)
        lcum = _cumsum_rows(consts[0], lf_ref[0, rows, :]) * LOG2_E
        lc_ref[slot] = lcum
        mild_ref[slot] = (jnp.min(lcum[chunk - 1:chunk, :]) >= -HGRN_MILD_LOG2).astype(jnp.int32)

    prepare(0, 0)

    def body(n, carry):
        rows = pl.ds(pl.multiple_of(n * chunk, chunk), chunk)
        slot = lax.rem(n, 2)
        mild = mild_ref[slot] == 1

        def run(step):
            for h in range(heads):
                lanes = slice(h * HEAD_DIM, (h + 1) * HEAD_DIM)
                o, st = step(q_ref[0, rows, lanes].astype(F32), k_ref[0, rows, lanes].astype(F32),
                             v_ref[0, rows, lanes].astype(F32), lc_ref[slot, :, lanes], st_ref[h])
                st_ref[h] = st
                o_ref[0, rows, lanes] = (_head_norm(o, gn) * sg_ref[0, rows, lanes].astype(F32)).astype(o_ref.dtype)
            prepare(jnp.minimum(n + 1, n_chunks - 1), 1 - slot)

        @pl.when(mild)
        def _():
            run(lambda q, k, v, g, st: _hgrn_chunk_mild(q, k, v, g, st, causal))

        @pl.when(jnp.logical_not(mild))
        def _():
            run(lambda q, k, v, g, st: _hgrn_chunk(q, k, v, g, st, consts, chunk, sub))

        return carry

    lax.fori_loop(0, n_chunks, body, 0)

    @pl.when(j == pl.num_programs(2) - 1)
    def _():
        for h in range(heads):
            s_ref[0, h] = st_ref[h].T


def _hgrn_prompt(q, k, v, lf, sg, gnorm, s0):
    bsz, t, d = q.shape
    tb = HGRN_BLOCK
    hg = HGRN_HEAD_GROUP
    blk = pl.BlockSpec((1, tb, hg * HEAD_DIM), lambda b, g, j: (b, j, g))
    sblk = pl.BlockSpec((1, hg, HEAD_DIM, HEAD_DIM), lambda b, g, j: (b, g, 0, 0))
    return pl.pallas_call(
        functools.partial(_hgrn_prompt_kernel, chunk=HGRN_CHUNK, sub=HGRN_SUB),
        grid=(bsz, HGRN_HEADS // hg, t // tb),
        in_specs=[blk, blk, blk, blk, blk, _full((1, HEAD_DIM)), sblk],
        out_specs=[blk, sblk],
        out_shape=[jax.ShapeDtypeStruct((bsz, t, d), BF16),
                   jax.ShapeDtypeStruct(s0.shape, F32)],
        scratch_shapes=[pltpu.VMEM((hg, HEAD_DIM, HEAD_DIM), F32),
                        pltpu.VMEM((2, HGRN_CHUNK, hg * HEAD_DIM), F32), pltpu.SMEM((2,), jnp.int32)],
        compiler_params=_params(("arbitrary", "arbitrary", "arbitrary")),
        name="hgrn_prompt",
    )(q, k, v, lf, sg, gnorm, s0)


def _hgrn_sample_kernel(q_ref, k_ref, v_ref, lf_ref, sg_ref, gn_ref, s0_ref, o_ref, s_ref):
    sb, t_new, d_model = q_ref.shape
    gn = gn_ref[...]
    pad_rows = SAMPLE_PAD - t_new - 1

    def body(b, carry):
        q, k, v, sg = q_ref[b], k_ref[b], v_ref[b], sg_ref[b]
        lf = lf_ref[b]
        lrow = []
        for t in range(t_new):
            lrow.append(lf[t:t + 1, :] if t == 0 else lrow[-1] + lf[t:t + 1, :])
        lcum = jnp.concatenate(lrow, axis=0)
        l_end = lrow[-1]
        qe = q * jnp.exp(lcum)
        ke = k * jnp.exp(l_end - lcum)
        f_end = jnp.exp(l_end)
        zeros = jnp.zeros((pad_rows, d_model), F32)
        zero_row = jnp.zeros((1, d_model), F32)
        qe_p = jnp.concatenate([qe, zero_row, zeros], axis=0)
        ke_p = jnp.concatenate([ke, f_end, zeros], axis=0)
        v_p = jnp.concatenate([v, zero_row, zeros], axis=0)
        for h in range(HGRN_HEADS):
            lanes = slice(h * HEAD_DIM, (h + 1) * HEAD_DIM)
            s0 = s0_ref[b, h]
            o = _dot(qe_p[:, lanes].astype(BF16), s0.astype(BF16))[0:t_new]
            rows = []
            for t in range(t_new):
                ot = None
                for s in range(t + 1):
                    prod = q[t:t + 1, lanes] * k[s:s + 1, lanes]
                    if s < t:
                        prod = prod * jnp.exp(lrow[t][:, lanes] - lrow[s][:, lanes])
                    term = jnp.sum(prod, axis=-1, keepdims=True) * v[s:s + 1, lanes]
                    ot = term if ot is None else ot + term
                rows.append(ot)
            o = o + jnp.concatenate(rows, axis=0)
            ke_t = ke_p[:, lanes].T
            decay = ke_t[:, t_new:t_new + 1]
            s_ref[b, h] = s0 * decay + _dot(ke_t.astype(BF16), v_p[:, lanes].astype(BF16))
            o_ref[b, :, lanes] = _head_norm(o, gn) * sg[:, lanes]
        return carry

    lax.fori_loop(0, sb, body, 0)


def _hgrn_sample(q, k, v, lf, sg, gnorm, s0):
    bsz, t, d = q.shape
    sb = HGRN_SAMPLE_BLOCK
    blk = pl.BlockSpec((sb, t, d), lambda g: (g, 0, 0))
    sblk = pl.BlockSpec((sb, HGRN_HEADS, HEAD_DIM, HEAD_DIM), lambda g: (g, 0, 0, 0))
    return pl.pallas_call(
        _hgrn_sample_kernel,
        grid=(bsz // sb,),
        in_specs=[blk, blk, blk, blk, blk, _full((1, HEAD_DIM)), sblk],
        out_specs=[blk, sblk],
        out_shape=[jax.ShapeDtypeStruct((bsz, t, d), F32),
                   jax.ShapeDtypeStruct(s0.shape, F32)],
        compiler_params=_params(("arbitrary",)),
        name="hgrn_sample",
    )(q, k, v, lf, sg, gnorm, s0)


def _merge_kernel(x_ref, ha_ref, hb_ref, ga_ref, gb_ref, wa_ref, wb_ref, wo_ref, g_ref, b_ref, o_ref):
    ya = _dot(ha_ref[...].astype(BF16), wa_ref[...])
    yb = _dot(hb_ref[...].astype(BF16), wb_ref[...])
    mixed = ga_ref[...].astype(F32) * ya + gb_ref[...].astype(F32) * yb
    z = DEEPNORM_ALPHA * x_ref[...] + _dot(mixed.astype(BF16), wo_ref[...])
    o_ref[...] = _layer_norm(z, g_ref[...], b_ref[...])


def _merge(x2d, ha, hb, ga, gb, wa, wb, wo, g, b):
    n, d = x2d.shape
    tm = MERGE_TOKEN_BLOCK if n % MERGE_TOKEN_BLOCK == 0 else TOKEN_BLOCK
    row = pl.BlockSpec((tm, d), lambda i: (i, 0))
    return pl.pallas_call(
        _merge_kernel,
        grid=(n // tm,),
        in_specs=[row] * 5 + [_full((d, d))] * 3 + [_full((1, d))] * 2,
        out_specs=row,
        out_shape=jax.ShapeDtypeStruct((n, d), F32),
        compiler_params=_params(("arbitrary",)),
        name="merge",
    )(x2d, ha, hb, ga, gb, wa, wb, wo, g, b)


def _route_t(s, bias):
    n = s.shape[1]
    neg = -jnp.inf
    sb = (s + bias).reshape(N_GROUPS, GROUP_SIZE, n)
    s3 = s.reshape(N_GROUPS, GROUP_SIZE, n)
    e_in_g = lax.broadcasted_iota(jnp.int32, sb.shape, 1)
    m1 = jnp.max(sb, axis=1, keepdims=True)
    first = jnp.min(jnp.where(sb == m1, e_in_g, GROUP_SIZE), axis=1, keepdims=True)
    m2 = jnp.max(jnp.where(e_in_g == first, neg, sb), axis=1, keepdims=True)
    gscore = (m1 + m2)[:, 0, :]
    gid = lax.broadcasted_iota(jnp.int32, gscore.shape, 0)
    gsel = jnp.zeros(gscore.shape, F32)
    for _ in range(TOPK_GROUPS):
        gm = jnp.max(gscore, axis=0, keepdims=True)
        pick = jnp.min(jnp.where(gscore == gm, gid, N_GROUPS), axis=0, keepdims=True)
        hit = gid == pick
        gsel = jnp.where(hit, 1.0, gsel)
        gscore = jnp.where(hit, neg, gscore)
    cand = jnp.where(gsel[:, None, :] > 0.5, sb, neg)
    eid = lax.broadcasted_iota(jnp.int32, sb.shape, 0) * GROUP_SIZE + e_in_g
    esel = jnp.zeros(sb.shape, F32)
    for _ in range(TOP_K):
        em = jnp.max(jnp.max(cand, axis=1, keepdims=True), axis=0, keepdims=True)
        masked = jnp.where(cand == em, eid, N_EXPERTS)
        pick = jnp.min(jnp.min(masked, axis=1, keepdims=True), axis=0, keepdims=True)
        hit = eid == pick
        esel = jnp.where(hit, 1.0, esel)
        cand = jnp.where(hit, neg, cand)
    w = esel * s3
    tot = jnp.sum(jnp.sum(w, axis=1, keepdims=True), axis=0, keepdims=True)
    return (w / tot * ROUTED_SCALE).reshape(N_EXPERTS, n), jnp.max(esel, axis=1)


def _split_bf16(a):
    hi = a.astype(BF16)
    return hi, (a - hi.astype(F32)).astype(BF16)


def _moe_kernel(x_ref, p_ref, wrt_ref, rb_ref, wg_ref, wu_ref, wd_ref, sg_ref, su_ref, sd_ref,
                g2_ref, b2_ref, pg_ref, pp_ref, o_ref,
                xb_ref, gt_ref, acc_ref, key_ref, ok_ref, xg_ref, gg_ref, pt_ref, accg_ref):
    j = pl.program_id(1)
    eb = wg_ref.shape[0]
    tm = x_ref.shape[0]
    n_sub, cap = xg_ref.shape[0], xg_ref.shape[1]
    ts = tm // n_sub
    steps_per_group = GROUP_SIZE // eb
    group = j // steps_per_group
    step_in_group = j - group * steps_per_group

    @pl.when(j == 0)
    def _():
        x = x_ref[...]
        logits = lax.dot_general(wrt_ref[...], x, (((1,), (1,)), ((), ())),
                                 precision=lax.Precision.HIGHEST, preferred_element_type=F32)
        gates, gsel = _route_t(_sigmoid(logits), rb_ref[...])
        gt_ref[...] = gates
        earlier = (lax.broadcasted_iota(jnp.int32, (ts, ts), 0)
                   < lax.broadcasted_iota(jnp.int32, (ts, ts), 1))
        before = jnp.where(earlier, 1.0, 0.0).astype(BF16)
        for s in range(n_sub):
            sel = gsel[:, s * ts:(s + 1) * ts]
            key = jnp.where(sel > 0.5, _dot(sel.astype(BF16), before), -1.0)
            for g in range(N_GROUPS):
                key_ref[s, g] = key[g:g + 1, :]
        xb = x.astype(BF16)
        xb_ref[...] = xb
        hs = _silu(_dot(xb, sg_ref[...])) * _dot(xb, su_ref[...])
        acc_ref[...] = _dot(hs.astype(BF16), sd_ref[...])

    slab = pl.multiple_of(group * GROUP_SIZE, GROUP_SIZE)
    in_group = step_in_group * eb

    def experts(rows, gate_cols, sink_ref, sink_rows):
        for r in range(eb):
            gate = gate_cols[:, r:r + 1]
            for off in range(eb, GROUP_SIZE, eb):
                gate = jnp.where(in_group == off, gate_cols[:, off + r:off + r + 1], gate)
            h = _silu(_dot(rows, wg_ref[r])) * _dot(rows, wu_ref[r])
            sink_ref[sink_rows] += _dot((h * gate).astype(BF16), wd_ref[r])

    for s in range(n_sub):
        tok = slice(s * ts, (s + 1) * ts)

        @pl.when(step_in_group == 0)
        def _():
            key_row = key_ref[s, group]
            count = jnp.sum(jnp.where(key_row >= 0.0, 1.0, 0.0))
            fits = count <= float(cap)
            ok_ref[s] = fits.astype(jnp.int32)

            @pl.when(fits)
            def _():
                slot = lax.broadcasted_iota(jnp.int32, (cap, ts), 0).astype(F32)
                pick = jnp.where(slot == key_row, 1.0, 0.0).astype(BF16)
                xg_ref[s] = _dot(pick, xb_ref[tok, :]).astype(BF16)
                g_rows = jnp.concatenate([gt_ref[pl.ds(slab, GROUP_SIZE), tok],
                                          jnp.zeros((LANES - GROUP_SIZE, ts), F32)], axis=0)
                g_hi, g_lo = _split_bf16(g_rows.T)
                gg_ref[s] = _dot(pick, g_hi) + _dot(pick, g_lo)
                key_col = jnp.broadcast_to(key_row, (SUBLANES, ts)).T[:, 0:1]
                slot_t = lax.broadcasted_iota(jnp.int32, (ts, cap), 1).astype(F32)
                pt_ref[s] = jnp.where(slot_t == key_col, 1.0, 0.0).astype(BF16)
                accg_ref[s] = jnp.zeros((cap, accg_ref.shape[2]), F32)

        compact = ok_ref[s] == 1

        @pl.when(compact)
        def _():
            experts(xg_ref[s], gg_ref[s], accg_ref, s)

        @pl.when(jnp.logical_not(compact))
        def _():
            experts(xb_ref[tok, :], gt_ref[pl.ds(slab, GROUP_SIZE), tok].T, acc_ref, tok)

        @pl.when(jnp.logical_and(compact, step_in_group == steps_per_group - 1))
        def _():
            acc_ref[tok, :] += _dot(pt_ref[s], accg_ref[s].astype(BF16))

    @pl.when(j == pl.num_programs(1) - 1)
    def _():
        x2 = _layer_norm(DEEPNORM_ALPHA * x_ref[...] + acc_ref[...], g2_ref[...], b2_ref[...])
        gate = _sigmoid(_dot(x2.astype(BF16), pg_ref[...]))
        o_ref[...] = x2 + gate * _dot(p_ref[...].astype(BF16), pp_ref[...])


def _moe(x2d, p2d, wrt, rbias, wg, wu, wd, sg, su, sd, g2, b2, pg, pp):
    n, d = x2d.shape
    tm = MOE_TOKEN_BLOCK if n % MOE_TOKEN_BLOCK == 0 else MOE_SUB_BLOCK
    ts = MOE_SUB_BLOCK
    n_sub = tm // ts
    eb = EXPERTS_PER_STEP
    cap = MOE_GROUP_CAP
    assert GROUP_SIZE % eb == 0 and n % tm == 0 and tm % ts == 0
    dp = p2d.shape[1]
    ff = wg.shape[2]
    sff = sg.shape[1]
    row = pl.BlockSpec((tm, d), lambda i, j: (i, 0))
    return pl.pallas_call(
        _moe_kernel,
        grid=(n // tm, N_EXPERTS // eb),
        in_specs=[row, pl.BlockSpec((tm, dp), lambda i, j: (i, 0)),
                  _full((N_EXPERTS, d)), _full((N_EXPERTS, 1)),
                  pl.BlockSpec((eb, d, ff), lambda i, j: (j, 0, 0)),
                  pl.BlockSpec((eb, d, ff), lambda i, j: (j, 0, 0)),
                  pl.BlockSpec((eb, ff, d), lambda i, j: (j, 0, 0)),
                  _full((d, sff)), _full((d, sff)), _full((sff, d)),
                  _full((1, d)), _full((1, d)), _full((d, d)), _full((dp, d))],
        out_specs=row,
        out_shape=jax.ShapeDtypeStruct((n, d), F32),
        scratch_shapes=[pltpu.VMEM((tm, d), BF16), pltpu.VMEM((N_EXPERTS, tm), F32),
                        pltpu.VMEM((tm, d), F32),
                        pltpu.VMEM((n_sub, N_GROUPS, 1, ts), F32), pltpu.SMEM((n_sub,), jnp.int32),
                        pltpu.VMEM((n_sub, cap, d), BF16), pltpu.VMEM((n_sub, cap, LANES), F32),
                        pltpu.VMEM((n_sub, ts, cap), BF16), pltpu.VMEM((n_sub, cap, d), F32)],
        compiler_params=_params(("arbitrary", "arbitrary")),
        name="moe",
    )(x2d, p2d, wrt, rbias, wg, wu, wd, sg, su, sd, g2, b2, pg, pp)


def kernel(x_prompt, x_sample, p_prompt, p_sample, state_conv, state_hgrn, w_in, b_in, hgrn_lb, conv_w, conv_b, conv_ln_g, conv_ln_b, w_conv_out, hgrn_norm_g, w_hgrn_out, w_o, ln1_g, ln1_b, w_router, router_bias, w_exp_gate, w_exp_up, w_exp_down, w_sh_gate, w_sh_up, w_sh_down, ln2_g, ln2_b, w_ple_gate, w_ple_proj):
    assert w_in.shape[0] == DEPTH == 1
    bp, tp, d = x_prompt.shape
    bs, ts, _ = x_sample.shape
    nbuf = CONV_WIDTH - 1
    i = 0
    layer = lambda a: a.reshape(a.shape[1:])
    row = lambda a: a.reshape(1, -1)
    bf = lambda a: layer(a).astype(BF16)

    w_in_b, wco, who, wo = bf(w_in), bf(w_conv_out), bf(w_hgrn_out), bf(w_o)
    weg, weu, wed = bf(w_exp_gate), bf(w_exp_up), bf(w_exp_down)
    wsg, wsu, wsd = bf(w_sh_gate), bf(w_sh_up), bf(w_sh_down)
    wpg, wpp = bf(w_ple_gate), bf(w_ple_proj)
    wrt = layer(w_router).T
    rbias = router_bias.reshape(N_EXPERTS, 1)
    gnorm = row(hgrn_norm_g)

    def tail(x2d, p2d, ha, hb, ga, gb):
        x1 = _merge(x2d, ha, hb, ga, gb, wco, who, wo, row(ln1_g), row(ln1_b))
        return _moe(x1, p2d, wrt, rbias, weg, weu, wed, wsg, wsu, wsd, row(ln2_g), row(ln2_b), wpg, wpp)

    xp = x_prompt.reshape(bp * tp, d)
    ha, u_tail, q, k, v, lf, sg, ga, gb = _proj_conv(xp, tp, w_in_b, row(b_in), hgrn_lb, i, layer(conv_w),
                                                     row(conv_b), row(conv_ln_g), row(conv_ln_b))
    seq = lambda a: a.reshape(bp, tp, d)
    hb, hgrn_p = _hgrn_prompt(seq(q), seq(k), seq(v), seq(lf), seq(sg), gnorm,
                              jnp.zeros((bp, HGRN_HEADS, HEAD_DIM, HEAD_DIM), F32))
    y_p = tail(xp, p_prompt.reshape(bp * tp, -1), ha, hb.reshape(bp * tp, d), ga, gb)
    conv_p = u_tail.reshape(bp, CONV_HALO, d)[:, CONV_HALO - nbuf:, :]

    tmaj = lambda a: jnp.swapaxes(a, 0, 1).reshape(ts * bs, -1)
    xs = tmaj(x_sample)
    u, q, k, v, lf, sg, ga, gb = _proj(xs, w_in_b, row(b_in), hgrn_lb, i)
    seq = lambda a: a.reshape(ts, bs, d)
    ha, conv_s = _conv_sample(seq(u), layer(state_conv), layer(conv_w), row(conv_b), row(conv_ln_g),
                              row(conv_ln_b))
    bmaj = lambda a: jnp.swapaxes(seq(a), 0, 1).astype(F32)
    hb, hgrn_s = _hgrn_sample(bmaj(q), bmaj(k), bmaj(v), bmaj(lf), bmaj(sg), gnorm, layer(state_hgrn))
    y_s = tail(xs, tmaj(layer(p_sample)), ha.reshape(ts * bs, d), tmaj(hb), ga, gb)
    y_s = jnp.swapaxes(y_s.reshape(ts, bs, d), 0, 1)

    return (y_p.reshape(bp, tp, d), y_s, conv_p[None], hgrn_p[None], conv_s[None], hgrn_s[None])
```

```python
import functools

import jax
import jax.numpy as jnp
from jax import lax
from jax.experimental import pallas as pl
from jax.experimental.pallas import tpu as pltpu

F32 = jnp.float32
BF16 = jnp.bfloat16

D_MODEL = 1024
CONV_WIDTH = 31
HGRN_HEADS = 8
HEAD_DIM = D_MODEL // HGRN_HEADS
N_EXPERTS = 64
N_GROUPS = 8
GROUP_SIZE = N_EXPERTS // N_GROUPS
TOPK_GROUPS = 4
TOP_K = 8
EXPERT_FF = 256
ROUTED_SCALE = 2.5
LN_EPS = 1e-5
LOG2_E = 1.4426950408889634
NEG_BIG = 1e30
SUBLANES = 8
LANES = 128
DEPTH = 1
DEEPNORM_ALPHA = (2.0 * DEPTH) ** 0.25

V7X_VMEM_BYTES = 64 * 1024 * 1024
VMEM_LIMIT = V7X_VMEM_BYTES - 8 * 1024 * 1024

TOKEN_BLOCK = 512
CONV_ROWS = 128
NORM_ROWS = 64
SAMPLE_SEQ_BLOCK = 32
CONV_HALO = 32
HGRN_BLOCK = 1024
HGRN_HEAD_GROUP = 8
HGRN_CHUNK = 64
HGRN_MILD_LOG2 = 96.0
HGRN_SUB = 8
HGRN_SAMPLE_BLOCK = 8
SAMPLE_PAD = 16
MOE_TOKEN_BLOCK = 1024
MOE_SUB_BLOCK = 512
MOE_GROUP_CAP = 288
MERGE_TOKEN_BLOCK = 1024
EXPERTS_PER_STEP = 4


def _sigmoid(x):
    return 1.0 / (1.0 + jnp.exp(-x))


def _silu(x):
    return x * _sigmoid(x)


def _layer_norm(x, g, b):
    mu = jnp.mean(x, axis=-1, keepdims=True)
    xc = x - mu
    var = jnp.mean(xc * xc, axis=-1, keepdims=True)
    return xc * lax.rsqrt(var + LN_EPS) * g + b


def _dot(a, b):
    return jnp.dot(a, b, preferred_element_type=F32)


def _dot_nt(a, b):
    return lax.dot_general(a, b, (((1,), (1,)), ((), ())), preferred_element_type=F32)


def _full(shape):
    return pl.BlockSpec(shape, lambda *_: (0,) * len(shape))


def _params(sem):
    return pltpu.CompilerParams(dimension_semantics=sem, vmem_limit_bytes=VMEM_LIMIT)


def _proj_columns(x_ref, w_ref, b_ref, lbp_ref, layer, emit_u, q_ref, k_ref, v_ref, lf_ref, sg_ref, ga_ref, gb_ref):
    D = D_MODEL
    xb = x_ref[...].astype(BF16)

    def col(j):
        return _dot(xb, w_ref[:, j * D:(j + 1) * D]) + b_ref[:, j * D:(j + 1) * D]

    emit_u(col(0) * _sigmoid(col(1)))
    hl = lbp_ref[...]
    e = jnp.exp(hl - jnp.max(hl, axis=0, keepdims=True))
    lb = jnp.sum(e[:layer + 1], axis=0, keepdims=True) / jnp.sum(e, axis=0, keepdims=True)
    fz = col(2)
    lf_ref[...] = jnp.log(lb + (1.0 - lb) * _sigmoid(fz))
    k_ref[...] = ((1.0 - lb) * _sigmoid(-fz)).astype(k_ref.dtype)
    v_ref[...] = col(3).astype(v_ref.dtype)
    q_ref[...] = _silu(col(4)).astype(q_ref.dtype)
    sg_ref[...] = _silu(col(5)).astype(sg_ref.dtype)
    ga_ref[...] = _sigmoid(col(6)).astype(ga_ref.dtype)
    gb_ref[...] = _sigmoid(col(7)).astype(gb_ref.dtype)


def _proj_kernel(x_ref, w_ref, b_ref, lbp_ref, u_ref, q_ref, k_ref, v_ref, lf_ref, sg_ref,
                 ga_ref, gb_ref, *, layer):
    def emit_u(u):
        u_ref[...] = u

    _proj_columns(x_ref, w_ref, b_ref, lbp_ref, layer, emit_u, q_ref, k_ref, v_ref, lf_ref, sg_ref, ga_ref, gb_ref)


def _proj(x2d, w_in, b_in, hgrn_lb, layer):
    n, d = x2d.shape
    tm = TOKEN_BLOCK
    cols = w_in.shape[1]
    row = pl.BlockSpec((tm, d), lambda i: (i, 0))
    outs = [jax.ShapeDtypeStruct((n, d), dt) for dt in (F32, BF16, BF16, BF16, F32, BF16, BF16, BF16)]
    return pl.pallas_call(
        functools.partial(_proj_kernel, layer=layer),
        grid=(n // tm,),
        in_specs=[row,
                  pl.BlockSpec((d, cols), lambda i: (0, 0), pipeline_mode=pl.Buffered(1)),
                  _full((1, cols)), _full(hgrn_lb.shape)],
        out_specs=[row] * 8,
        out_shape=outs,
        compiler_params=_params(("arbitrary",)),
        name="proj",
    )(x2d, w_in, b_in, hgrn_lb)


def _conv_tap_groups():
    shift = CONV_HALO - (CONV_WIDTH - 1)
    groups = [[] for _ in range(SUBLANES)]
    for j in range(CONV_WIDTH):
        groups[(j + shift) % SUBLANES].append((j, (j + shift) // SUBLANES))
    return groups


def _conv_rows(ext_ref, cw_ref, cb_ref, y_ref, base, rows):
    for l in range(D_MODEL // LANES):
        lanes = slice(l * LANES, (l + 1) * LANES)
        acc = None
        for res, taps in enumerate(_conv_tap_groups()):
            part = None
            for j, a in taps:
                term = cw_ref[j:j + 1, lanes] * ext_ref[pl.ds(base + SUBLANES * a, rows + SUBLANES), lanes]
                part = term if part is None else part + term
            part = part[res:res + rows, :]
            acc = part if acc is None else acc + part
        y_ref[pl.ds(base, rows), lanes] = acc + cb_ref[:, lanes]


def _proj_conv_kernel(x_ref, w_ref, b_ref, lbp_ref, cw_ref, cb_ref, g_ref, bb_ref,
                      h_ref, tail_ref, q_ref, k_ref, v_ref, lf_ref, sg_ref, ga_ref, gb_ref,
                      ext_ref, y_ref, *, layer):
    tm = x_ref.shape[0]

    @pl.when(pl.program_id(1) == 0)
    def _():
        ext_ref[0:CONV_HALO, :] = jnp.zeros((CONV_HALO, D_MODEL), F32)
        ext_ref[CONV_HALO + tm:CONV_HALO + tm + SUBLANES, :] = jnp.zeros((SUBLANES, D_MODEL), F32)

    def emit_u(u):
        ext_ref[CONV_HALO:CONV_HALO + tm, :] = u
        tail_ref[...] = u[tm - CONV_HALO:tm, :]
        for r in range(tm // CONV_ROWS):
            _conv_rows(ext_ref, cw_ref, cb_ref, y_ref, r * CONV_ROWS, CONV_ROWS)
        for r in range(tm // NORM_ROWS):
            rows = slice(r * NORM_ROWS, (r + 1) * NORM_ROWS)
            h_ref[rows, :] = _silu(_layer_norm(y_ref[rows, :], g_ref[...], bb_ref[...])).astype(h_ref.dtype)
        ext_ref[0:CONV_HALO, :] = ext_ref[tm:tm + CONV_HALO, :]

    _proj_columns(x_ref, w_ref, b_ref, lbp_ref, layer, emit_u, q_ref, k_ref, v_ref, lf_ref, sg_ref, ga_ref, gb_ref)


def _proj_conv(x2d, seq_len, w_in, b_in, hgrn_lb, layer, conv_w, conv_b, g, b):
    n, d = x2d.shape
    tm = TOKEN_BLOCK
    nt = seq_len // tm
    cols = w_in.shape[1]
    row = pl.BlockSpec((tm, d), lambda s, j: (s * nt + j, 0))
    outs = [jax.ShapeDtypeStruct((n, d), BF16), jax.ShapeDtypeStruct((n // seq_len * CONV_HALO, d), F32)]
    outs += [jax.ShapeDtypeStruct((n, d), dt) for dt in (BF16, BF16, BF16, F32, BF16, BF16, BF16)]
    return pl.pallas_call(
        functools.partial(_proj_conv_kernel, layer=layer),
        grid=(n // seq_len, nt),
        in_specs=[row,
                  pl.BlockSpec((d, cols), lambda s, j: (0, 0), pipeline_mode=pl.Buffered(1)),
                  _full((1, cols)), _full(hgrn_lb.shape),
                  _full(conv_w.shape), _full((1, d)), _full((1, d)), _full((1, d))],
        out_specs=[row, pl.BlockSpec((CONV_HALO, d), lambda s, j: (s, 0))] + [row] * 7,
        out_shape=outs,
        scratch_shapes=[pltpu.VMEM((CONV_HALO + tm + SUBLANES, d), F32), pltpu.VMEM((tm, d), F32)],
        compiler_params=_params(("arbitrary", "arbitrary")),
        name="proj_conv",
    )(x2d, w_in, b_in, hgrn_lb, conv_w, conv_b, g, b)


def _conv_sample_kernel(u_ref, st_ref, cw_ref, cb_ref, g_ref, b_ref, h_ref, new_ref):
    t_new, sb, d = u_ref.shape
    nbuf = st_ref.shape[1]

    def ext_row(r, rows):
        if r < nbuf:
            return st_ref[rows, r, :]
        return u_ref[r - nbuf, rows, :]

    def body(gi, carry):
        rows = pl.ds(pl.multiple_of(gi * SUBLANES, SUBLANES), SUBLANES)
        for t in range(t_new):
            acc = jnp.zeros((SUBLANES, d), F32) + cb_ref[...]
            for j in range(CONV_WIDTH):
                acc = acc + cw_ref[j:j + 1, :] * ext_row(t + j, rows)
            h_ref[t, rows, :] = _silu(_layer_norm(acc, g_ref[...], b_ref[...])).astype(h_ref.dtype)
        return carry

    lax.fori_loop(0, sb // SUBLANES, body, 0)
    new_ref[:, 0:nbuf - t_new, :] = st_ref[:, t_new:nbuf, :]
    for t in range(t_new):
        new_ref[:, nbuf - t_new + t, :] = u_ref[t]


def _conv_sample(u, state, conv_w, conv_b, g, b):
    t, bsz, d = u.shape
    nbuf = state.shape[1]
    sb = SAMPLE_SEQ_BLOCK
    return pl.pallas_call(
        _conv_sample_kernel,
        grid=(bsz // sb,),
        in_specs=[pl.BlockSpec((t, sb, d), lambda i: (0, i, 0)),
                  pl.BlockSpec((sb, nbuf, d), lambda i: (i, 0, 0)),
                  _full(conv_w.shape), _full((1, d)), _full((1, d)), _full((1, d))],
        out_specs=[pl.BlockSpec((t, sb, d), lambda i: (0, i, 0)),
                   pl.BlockSpec((sb, nbuf, d), lambda i: (i, 0, 0))],
        out_shape=[jax.ShapeDtypeStruct((t, bsz, d), F32),
                   jax.ShapeDtypeStruct((bsz, nbuf, d), F32)],
        compiler_params=_params(("arbitrary",)),
        name="conv_sample",
    )(u, state, conv_w, conv_b, g, b)


def _hgrn_chunk(q, k, v, g, st, consts, chunk, sub):
    _, blk_mask, neg_masks = consts
    nb = chunk // sub
    lcum = g
    l_end = lcum[chunk - 1:chunk, :]

    o = _dot_nt((q * jnp.exp2(lcum)).astype(BF16), st.astype(BF16))

    if nb > 1:
        starts = [lcum[i * sub - 1:i * sub, :] for i in range(1, nb)]
        l_start = jnp.concatenate(
            [jnp.zeros((sub, HEAD_DIM), F32)]
            + [jnp.broadcast_to(s, (sub, HEAD_DIM)) for s in starts], axis=0)
        q_rel = (q * jnp.exp2(lcum - l_start)).astype(BF16)
        k_stack = jnp.concatenate(
            [k[0:i * sub] * jnp.exp2(starts[i - 1] - lcum[0:i * sub]) for i in range(1, nb)],
            axis=0).astype(BF16)
        v_stack = jnp.concatenate([v[0:i * sub] for i in range(1, nb)], axis=0).astype(BF16)
        scores = _dot_nt(q_rel, k_stack) * blk_mask
        o = o + _dot(scores.astype(BF16), v_stack)

    diag = []
    for i in range(nb):
        sl = slice(i * sub, (i + 1) * sub)
        qi, ki, li = q[sl], k[sl], lcum[sl]
        od = None
        for s in range(sub):
            dec = jnp.exp2(li - li[s:s + 1, :] + neg_masks[s])
            score = jnp.sum(qi * (ki[s:s + 1, :] * dec), axis=-1, keepdims=True)
            term = score * v[i * sub + s:i * sub + s + 1, :]
            od = term if od is None else od + term
        diag.append(od)
    o = o + (jnp.concatenate(diag, axis=0) if nb > 1 else diag[0])

    k_end = (k * jnp.exp2(l_end - lcum)).astype(BF16)
    st_new = st * jnp.exp2(l_end) + _dot(v.T.astype(BF16), k_end)
    return o, st_new


def _hgrn_chunk_mild(q, k, v, g, st, causal):
    chunk = q.shape[0]
    l_end = g[chunk - 1:chunk, :]
    q_dec = (q * jnp.exp2(g)).astype(BF16)
    k_inv = (k * jnp.exp2(-g)).astype(BF16)
    scores = jnp.where(causal, _dot_nt(q_dec, k_inv), 0.0).astype(BF16)
    o = _dot(jnp.concatenate([q_dec, scores], axis=1),
             jnp.concatenate([st.T.astype(BF16), v.astype(BF16)], axis=0))
    st_new = (st + _dot(v.T.astype(BF16), k_inv)) * jnp.exp2(l_end)
    return o, st_new


def _hgrn_consts(chunk, sub):
    nb = chunk // sub
    r = lax.broadcasted_iota(jnp.int32, (chunk, chunk), 0)
    c = lax.broadcasted_iota(jnp.int32, (chunk, chunk), 1)
    tri = jnp.where(c <= r, 1.0, 0.0).astype(BF16)
    row = lax.broadcasted_iota(jnp.int32, (sub, HEAD_DIM), 0)
    neg_masks = [jnp.where(row >= s, 0.0, -NEG_BIG).astype(F32) for s in range(sub)]
    blk_mask = None
    if nb > 1:
        width = sub * nb * (nb - 1) // 2
        rb = lax.broadcasted_iota(jnp.int32, (chunk, width), 0) // sub
        cc = lax.broadcasted_iota(jnp.int32, (chunk, width), 1)
        blk_mask = jnp.zeros((chunk, width), F32)
        off = 0
        for i in range(1, nb):
            hit = jnp.where(rb == i, jnp.where(cc >= off, jnp.where(cc < off + i * sub, 1.0, 0.0), 0.0), 0.0)
            blk_mask = blk_mask + hit
            off += i * sub
    return tri, blk_mask, neg_masks


def _cumsum_rows(tri_bf, g):
    g1 = g.astype(BF16)
    r1 = g - g1.astype(F32)
    g2 = r1.astype(BF16)
    g3 = (r1 - g2.astype(F32)).astype(BF16)
    return _dot(tri_bf, g1) + _dot(tri_bf, g2) + _dot(tri_bf, g3)


def _head_norm(o, gn):
    return o * lax.rsqrt(jnp.mean(o * o, axis=-1, keepdims=True) + LN_EPS) * gn


def _hgrn_prompt_kernel(q_ref, k_ref, v_ref, lf_ref, sg_ref, gn_ref, s0_ref, o_ref, s_ref,
                        st_ref, lc_ref, mild_ref, *, chunk, sub):
    tb = q_ref.shape[1]
    n_chunks = tb // chunk
    heads = q_ref.shape[2] // HEAD_DIM
    j = pl.program_id(2)
    consts = _hgrn_consts(chunk, sub)
    gn = gn_ref[...]

    @pl.when(j == 0)
    def _():
        for h in range(heads):
            st_ref[h] = s0_ref[0, h].T

    causal = (lax.broadcasted_iota(jnp.int32, (chunk, chunk), 1)
              <= lax.broadcasted_iota(jnp.int32, (chunk, chunk), 0))

    def prepare(n, slot):
        rows = pl.ds(pl.multiple_of(n * chunk, chunk), chunk)
        lcum = _cumsum_rows(consts[0], lf_ref[0, rows, :]) * LOG2_E
        lc_ref[slot] = lcum
        mild_ref[slot] = (jnp.min(lcum[chunk - 1:chunk, :]) >= -HGRN_MILD_LOG2).astype(jnp.int32)

    prepare(0, 0)

    def body(n, carry):
        rows = pl.ds(pl.multiple_of(n * chunk, chunk), chunk)
        slot = lax.rem(n, 2)
        mild = mild_ref[slot] == 1

        def run(step):
            for h in range(heads):
                lanes = slice(h * HEAD_DIM, (h + 1) * HEAD_DIM)
                o, st = step(q_ref[0, rows, lanes].astype(F32), k_ref[0, rows, lanes].astype(F32),
                             v_ref[0, rows, lanes].astype(F32), lc_ref[slot, :, lanes], st_ref[h])
                st_ref[h] = st
                o_ref[0, rows, lanes] = (_head_norm(o, gn) * sg_ref[0, rows, lanes].astype(F32)).astype(o_ref.dtype)
            prepare(jnp.minimum(n + 1, n_chunks - 1), 1 - slot)

        @pl.when(mild)
        def _():
            run(lambda q, k, v, g, st: _hgrn_chunk_mild(q, k, v, g, st, causal))

        @pl.when(jnp.logical_not(mild))
        def _():
            run(lambda q, k, v, g, st: _hgrn_chunk(q, k, v, g, st, consts, chunk, sub))

        return carry

    lax.fori_loop(0, n_chunks, body, 0)

    @pl.when(j == pl.num_programs(2) - 1)
    def _():
        for h in range(heads):
            s_ref[0, h] = st_ref[h].T


def _hgrn_prompt(q, k, v, lf, sg, gnorm, s0):
    bsz, t, d = q.shape
    tb = HGRN_BLOCK
    hg = HGRN_HEAD_GROUP
    blk = pl.BlockSpec((1, tb, hg * HEAD_DIM), lambda b, g, j: (b, j, g))
    sblk = pl.BlockSpec((1, hg, HEAD_DIM, HEAD_DIM), lambda b, g, j: (b, g, 0, 0))
    return pl.pallas_call(
        functools.partial(_hgrn_prompt_kernel, chunk=HGRN_CHUNK, sub=HGRN_SUB),
        grid=(bsz, HGRN_HEADS // hg, t // tb),
        in_specs=[blk, blk, blk, blk, blk, _full((1, HEAD_DIM)), sblk],
        out_specs=[blk, sblk],
        out_shape=[jax.ShapeDtypeStruct((bsz, t, d), BF16),
                   jax.ShapeDtypeStruct(s0.shape, F32)],
        scratch_shapes=[pltpu.VMEM((hg, HEAD_DIM, HEAD_DIM), F32),
                        pltpu.VMEM((2, HGRN_CHUNK, hg * HEAD_DIM), F32), pltpu.SMEM((2,), jnp.int32)],
        compiler_params=_params(("arbitrary", "arbitrary", "arbitrary")),
        name="hgrn_prompt",
    )(q, k, v, lf, sg, gnorm, s0)


def _hgrn_sample_kernel(q_ref, k_ref, v_ref, lf_ref, sg_ref, gn_ref, s0_ref, o_ref, s_ref):
    sb, t_new, d_model = q_ref.shape
    gn = gn_ref[...]
    pad_rows = SAMPLE_PAD - t_new - 1

    def body(b, carry):
        q, k, v, sg = q_ref[b], k_ref[b], v_ref[b], sg_ref[b]
        lf = lf_ref[b]
        lrow = []
        for t in range(t_new):
            lrow.append(lf[t:t + 1, :] if t == 0 else lrow[-1] + lf[t:t + 1, :])
        lcum = jnp.concatenate(lrow, axis=0)
        l_end = lrow[-1]
        qe = q * jnp.exp(lcum)
        ke = k * jnp.exp(l_end - lcum)
        f_end = jnp.exp(l_end)
        zeros = jnp.zeros((pad_rows, d_model), F32)
        zero_row = jnp.zeros((1, d_model), F32)
        qe_p = jnp.concatenate([qe, zero_row, zeros], axis=0)
        ke_p = jnp.concatenate([ke, f_end, zeros], axis=0)
        v_p = jnp.concatenate([v, zero_row, zeros], axis=0)
        for h in range(HGRN_HEADS):
            lanes = slice(h * HEAD_DIM, (h + 1) * HEAD_DIM)
            s0 = s0_ref[b, h]
            o = _dot(qe_p[:, lanes].astype(BF16), s0.astype(BF16))[0:t_new]
            rows = []
            for t in range(t_new):
                ot = None
                for s in range(t + 1):
                    prod = q[t:t + 1, lanes] * k[s:s + 1, lanes]
                    if s < t:
                        prod = prod * jnp.exp(lrow[t][:, lanes] - lrow[s][:, lanes])
                    term = jnp.sum(prod, axis=-1, keepdims=True) * v[s:s + 1, lanes]
                    ot = term if ot is None else ot + term
                rows.append(ot)
            o = o + jnp.concatenate(rows, axis=0)
            ke_t = ke_p[:, lanes].T
            decay = ke_t[:, t_new:t_new + 1]
            s_ref[b, h] = s0 * decay + _dot(ke_t.astype(BF16), v_p[:, lanes].astype(BF16))
            o_ref[b, :, lanes] = _head_norm(o, gn) * sg[:, lanes]
        return carry

    lax.fori_loop(0, sb, body, 0)


def _hgrn_sample(q, k, v, lf, sg, gnorm, s0):
    bsz, t, d = q.shape
    sb = HGRN_SAMPLE_BLOCK
    blk = pl.BlockSpec((sb, t, d), lambda g: (g, 0, 0))
    sblk = pl.BlockSpec((sb, HGRN_HEADS, HEAD_DIM, HEAD_DIM), lambda g: (g, 0, 0, 0))
    return pl.pallas_call(
        _hgrn_sample_kernel,
        grid=(bsz // sb,),
        in_specs=[blk, blk, blk, blk, blk, _full((1, HEAD_DIM)), sblk],
        out_specs=[blk, sblk],
        out_shape=[jax.ShapeDtypeStruct((bsz, t, d), F32),
                   jax.ShapeDtypeStruct(s0.shape, F32)],
        compiler_params=_params(("arbitrary",)),
        name="hgrn_sample",
    )(q, k, v, lf, sg, gnorm, s0)


def _merge_kernel(x_ref, ha_ref, hb_ref, ga_ref, gb_ref, wa_ref, wb_ref, wo_ref, g_ref, b_ref, o_ref):
    ya = _dot(ha_ref[...].astype(BF16), wa_ref[...])
    yb = _dot(hb_ref[...].astype(BF16), wb_ref[...])
    mixed = ga_ref[...].astype(F32) * ya + gb_ref[...].astype(F32) * yb
    z = DEEPNORM_ALPHA * x_ref[...] + _dot(mixed.astype(BF16), wo_ref[...])
    o_ref[...] = _layer_norm(z, g_ref[...], b_ref[...])


def _merge(x2d, ha, hb, ga, gb, wa, wb, wo, g, b):
    n, d = x2d.shape
    tm = MERGE_TOKEN_BLOCK if n % MERGE_TOKEN_BLOCK == 0 else TOKEN_BLOCK
    row = pl.BlockSpec((tm, d), lambda i: (i, 0))
    return pl.pallas_call(
        _merge_kernel,
        grid=(n // tm,),
        in_specs=[row] * 5 + [_full((d, d))] * 3 + [_full((1, d))] * 2,
        out_specs=row,
        out_shape=jax.ShapeDtypeStruct((n, d), F32),
        compiler_params=_params(("arbitrary",)),
        name="merge",
    )(x2d, ha, hb, ga, gb, wa, wb, wo, g, b)


def _route_t(s, bias):
    n = s.shape[1]
    neg = -jnp.inf
    sb = (s + bias).reshape(N_GROUPS, GROUP_SIZE, n)
    s3 = s.reshape(N_GROUPS, GROUP_SIZE, n)
    e_in_g = lax.broadcasted_iota(jnp.int32, sb.shape, 1)
    m1 = jnp.max(sb, axis=1, keepdims=True)
    first = jnp.min(jnp.where(sb == m1, e_in_g, GROUP_SIZE), axis=1, keepdims=True)
    m2 = jnp.max(jnp.where(e_in_g == first, neg, sb), axis=1, keepdims=True)
    gscore = (m1 + m2)[:, 0, :]
    gid = lax.broadcasted_iota(jnp.int32, gscore.shape, 0)
    gsel = jnp.zeros(gscore.shape, F32)
    for _ in range(TOPK_GROUPS):
        gm = jnp.max(gscore, axis=0, keepdims=True)
        pick = jnp.min(jnp.where(gscore == gm, gid, N_GROUPS), axis=0, keepdims=True)
        hit = gid == pick
        gsel = jnp.where(hit, 1.0, gsel)
        gscore = jnp.where(hit, neg, gscore)
    cand = jnp.where(gsel[:, None, :] > 0.5, sb, neg)
    eid = lax.broadcasted_iota(jnp.int32, sb.shape, 0) * GROUP_SIZE + e_in_g
    esel = jnp.zeros(sb.shape, F32)
    for _ in range(TOP_K):
        em = jnp.max(jnp.max(cand, axis=1, keepdims=True), axis=0, keepdims=True)
        masked = jnp.where(cand == em, eid, N_EXPERTS)
        pick = jnp.min(jnp.min(masked, axis=1, keepdims=True), axis=0, keepdims=True)
        hit = eid == pick
        esel = jnp.where(hit, 1.0, esel)
        cand = jnp.where(hit, neg, cand)
    w = esel * s3
    tot = jnp.sum(jnp.sum(w, axis=1, keepdims=True), axis=0, keepdims=True)
    return (w / tot * ROUTED_SCALE).reshape(N_EXPERTS, n), jnp.max(esel, axis=1)


def _split_bf16(a):
    hi = a.astype(BF16)
    return hi, (a - hi.astype(F32)).astype(BF16)


def _moe_kernel(x_ref, p_ref, wrt_ref, rb_ref, wg_ref, wu_ref, wd_ref, sg_ref, su_ref, sd_ref,
                g2_ref, b2_ref, pg_ref, pp_ref, o_ref,
                xb_ref, gt_ref, acc_ref, key_ref, ok_ref, xg_ref, gg_ref, pt_ref, accg_ref):
    j = pl.program_id(1)
    eb = wg_ref.shape[0]
    tm = x_ref.shape[0]
    n_sub, cap = xg_ref.shape[0], xg_ref.shape[1]
    ts = tm // n_sub
    steps_per_group = GROUP_SIZE // eb
    group = j // steps_per_group
    step_in_group = j - group * steps_per_group

    @pl.when(j == 0)
    def _():
        x = x_ref[...]
        logits = lax.dot_general(wrt_ref[...], x, (((1,), (1,)), ((), ())),
                                 precision=lax.Precision.HIGHEST, preferred_element_type=F32)
        gates, gsel = _route_t(_sigmoid(logits), rb_ref[...])
        gt_ref[...] = gates
        earlier = (lax.broadcasted_iota(jnp.int32, (ts, ts), 0)
                   < lax.broadcasted_iota(jnp.int32, (ts, ts), 1))
        before = jnp.where(earlier, 1.0, 0.0).astype(BF16)
        for s in range(n_sub):
            sel = gsel[:, s * ts:(s + 1) * ts]
            key = jnp.where(sel > 0.5, _dot(sel.astype(BF16), before), -1.0)
            for g in range(N_GROUPS):
                key_ref[s, g] = key[g:g + 1, :]
        xb = x.astype(BF16)
        xb_ref[...] = xb
        hs = _silu(_dot(xb, sg_ref[...])) * _dot(xb, su_ref[...])
        acc_ref[...] = _dot(hs.astype(BF16), sd_ref[...])

    slab = pl.multiple_of(group * GROUP_SIZE, GROUP_SIZE)
    in_group = step_in_group * eb

    def experts(rows, gate_cols, add):
        for r in range(eb):
            gate = gate_cols[:, r:r + 1]
            for off in range(eb, GROUP_SIZE, eb):
                gate = jnp.where(in_group == off, gate_cols[:, off + r:off + r + 1], gate)
            h = _silu(_dot(rows, wg_ref[r])) * _dot(rows, wu_ref[r])
            add(_dot((h * gate).astype(BF16), wd_ref[r]))

    def add_compact(s):
        def add(y):
            accg_ref[s] += y
        return add

    def add_dense(tok):
        def add(y):
            acc_ref[tok, :] += y
        return add

    def add_all_compact(y):
        accg_ref[...] += y.reshape(accg_ref.shape)

    for s in range(n_sub):
        tok = slice(s * ts, (s + 1) * ts)

        @pl.when(step_in_group == 0)
        def _():
            key_row = key_ref[s, group]
            count = jnp.sum(jnp.where(key_row >= 0.0, 1.0, 0.0))
            fits = count <= float(cap)
            ok_ref[s] = fits.astype(jnp.int32)

            @pl.when(fits)
            def _():
                slot = lax.broadcasted_iota(jnp.int32, (cap, ts), 0).astype(F32)
                pick = jnp.where(slot == key_row, 1.0, 0.0).astype(BF16)
                xg_ref[s] = _dot(pick, xb_ref[tok, :]).astype(BF16)
                g_rows = jnp.concatenate([gt_ref[pl.ds(slab, GROUP_SIZE), tok],
                                          jnp.zeros((LANES - GROUP_SIZE, ts), F32)], axis=0)
                g_hi, g_lo = _split_bf16(g_rows.T)
                gg_ref[s] = _dot(pick, g_hi) + _dot(pick, g_lo)
                key_col = jnp.broadcast_to(key_row, (SUBLANES, ts)).T[:, 0:1]
                slot_t = lax.broadcasted_iota(jnp.int32, (ts, cap), 1).astype(F32)
                pt_ref[s] = jnp.where(slot_t == key_col, 1.0, 0.0).astype(BF16)
                accg_ref[s] = jnp.zeros((cap, accg_ref.shape[2]), F32)

    flags = [ok_ref[s] == 1 for s in range(n_sub)]
    all_compact = functools.reduce(jnp.logical_and, flags)

    @pl.when(all_compact)
    def _():
        experts(xg_ref[...].reshape(n_sub * cap, xg_ref.shape[2]),
                gg_ref[...].reshape(n_sub * cap, LANES), add_all_compact)

    for s in range(n_sub):
        tok = slice(s * ts, (s + 1) * ts)
        alone = jnp.logical_and(flags[s], jnp.logical_not(all_compact))

        @pl.when(alone)
        def _():
            experts(xg_ref[s], gg_ref[s], add_compact(s))

        @pl.when(jnp.logical_not(flags[s]))
        def _():
            experts(xb_ref[tok, :], gt_ref[pl.ds(slab, GROUP_SIZE), tok].T, add_dense(tok))

        @pl.when(jnp.logical_and(flags[s], step_in_group == steps_per_group - 1))
        def _():
            acc_ref[tok, :] += _dot(pt_ref[s], accg_ref[s].astype(BF16))

    @pl.when(j == pl.num_programs(1) - 1)
    def _():
        x2 = _layer_norm(DEEPNORM_ALPHA * x_ref[...] + acc_ref[...], g2_ref[...], b2_ref[...])
        gate = _sigmoid(_dot(x2.astype(BF16), pg_ref[...]))
        o_ref[...] = x2 + gate * _dot(p_ref[...].astype(BF16), pp_ref[...])


def _moe(x2d, p2d, wrt, rbias, wg, wu, wd, sg, su, sd, g2, b2, pg, pp):
    n, d = x2d.shape
    tm = MOE_TOKEN_BLOCK if n % MOE_TOKEN_BLOCK == 0 else MOE_SUB_BLOCK
    ts = MOE_SUB_BLOCK
    n_sub = tm // ts
    eb = EXPERTS_PER_STEP
    cap = MOE_GROUP_CAP
    assert GROUP_SIZE % eb == 0 and n % tm == 0 and tm % ts == 0
    dp = p2d.shape[1]
    ff = wg.shape[2]
    sff = sg.shape[1]
    row = pl.BlockSpec((tm, d), lambda i, j: (i, 0))
    return pl.pallas_call(
        _moe_kernel,
        grid=(n // tm, N_EXPERTS // eb),
        in_specs=[row, pl.BlockSpec((tm, dp), lambda i, j: (i, 0)),
                  _full((N_EXPERTS, d)), _full((N_EXPERTS, 1)),
                  pl.BlockSpec((eb, d, ff), lambda i, j: (j, 0, 0)),
                  pl.BlockSpec((eb, d, ff), lambda i, j: (j, 0, 0)),
                  pl.BlockSpec((eb, ff, d), lambda i, j: (j, 0, 0)),
                  _full((d, sff)), _full((d, sff)), _full((sff, d)),
                  _full((1, d)), _full((1, d)), _full((d, d)), _full((dp, d))],
        out_specs=row,
        out_shape=jax.ShapeDtypeStruct((n, d), F32),
        scratch_shapes=[pltpu.VMEM((tm, d), BF16), pltpu.VMEM((N_EXPERTS, tm), F32),
                        pltpu.VMEM((tm, d), F32),
                        pltpu.VMEM((n_sub, N_GROUPS, 1, ts), F32), pltpu.SMEM((n_sub,), jnp.int32),
                        pltpu.VMEM((n_sub, cap, d), BF16), pltpu.VMEM((n_sub, cap, LANES), F32),
                        pltpu.VMEM((n_sub, ts, cap), BF16), pltpu.VMEM((n_sub, cap, d), F32)],
        compiler_params=_params(("arbitrary", "arbitrary")),
        name="moe",
    )(x2d, p2d, wrt, rbias, wg, wu, wd, sg, su, sd, g2, b2, pg, pp)


def kernel(x_prompt, x_sample, p_prompt, p_sample, state_conv, state_hgrn, w_in, b_in, hgrn_lb, conv_w, conv_b, conv_ln_g, conv_ln_b, w_conv_out, hgrn_norm_g, w_hgrn_out, w_o, ln1_g, ln1_b, w_router, router_bias, w_exp_gate, w_exp_up, w_exp_down, w_sh_gate, w_sh_up, w_sh_down, ln2_g, ln2_b, w_ple_gate, w_ple_proj):
    assert w_in.shape[0] == DEPTH == 1
    bp, tp, d = x_prompt.shape
    bs, ts, _ = x_sample.shape
    nbuf = CONV_WIDTH - 1
    i = 0
    layer = lambda a: a.reshape(a.shape[1:])
    row = lambda a: a.reshape(1, -1)
    bf = lambda a: layer(a).astype(BF16)

    w_in_b, wco, who, wo = bf(w_in), bf(w_conv_out), bf(w_hgrn_out), bf(w_o)
    weg, weu, wed = bf(w_exp_gate), bf(w_exp_up), bf(w_exp_down)
    wsg, wsu, wsd = bf(w_sh_gate), bf(w_sh_up), bf(w_sh_down)
    wpg, wpp = bf(w_ple_gate), bf(w_ple_proj)
    wrt = layer(w_router).T
    rbias = router_bias.reshape(N_EXPERTS, 1)
    gnorm = row(hgrn_norm_g)

    def tail(x2d, p2d, ha, hb, ga, gb):
        x1 = _merge(x2d, ha, hb, ga, gb, wco, who, wo, row(ln1_g), row(ln1_b))
        return _moe(x1, p2d, wrt, rbias, weg, weu, wed, wsg, wsu, wsd, row(ln2_g), row(ln2_b), wpg, wpp)

    xp = x_prompt.reshape(bp * tp, d)
    ha, u_tail, q, k, v, lf, sg, ga, gb = _proj_conv(xp, tp, w_in_b, row(b_in), hgrn_lb, i, layer(conv_w),
                                                     row(conv_b), row(conv_ln_g), row(conv_ln_b))
    seq = lambda a: a.reshape(bp, tp, d)
    hb, hgrn_p = _hgrn_prompt(seq(q), seq(k), seq(v), seq(lf), seq(sg), gnorm,
                              jnp.zeros((bp, HGRN_HEADS, HEAD_DIM, HEAD_DIM), F32))
    y_p = tail(xp, p_prompt.reshape(bp * tp, -1), ha, hb.reshape(bp * tp, d), ga, gb)
    conv_p = u_tail.reshape(bp, CONV_HALO, d)[:, CONV_HALO - nbuf:, :]

    tmaj = lambda a: jnp.swapaxes(a, 0, 1).reshape(ts * bs, -1)
    xs = tmaj(x_sample)
    u, q, k, v, lf, sg, ga, gb = _proj(xs, w_in_b, row(b_in), hgrn_lb, i)
    seq = lambda a: a.reshape(ts, bs, d)
    ha, conv_s = _conv_sample(seq(u), layer(state_conv), layer(conv_w), row(conv_b), row(conv_ln_g),
                              row(conv_ln_b))
    bmaj = lambda a: jnp.swapaxes(seq(a), 0, 1).astype(F32)
    hb, hgrn_s = _hgrn_sample(bmaj(q), bmaj(k), bmaj(v), bmaj(lf), bmaj(sg), gnorm, layer(state_hgrn))
    y_s = tail(xs, tmaj(layer(p_sample)), ha.reshape(ts * bs, d), tmaj(hb), ga, gb)
    y_s = jnp.swapaxes(y_s.reshape(ts, bs, d), 0, 1)

    return (y_p.reshape(bp, tp, d), y_s, conv_p[None], hgrn_p[None], conv_s[None], hgrn_s[None])
```

```python
import functools

import jax
import jax.numpy as jnp
from jax import lax
from jax.experimental import pallas as pl
from jax.experimental.pallas import tpu as pltpu

F32 = jnp.float32
BF16 = jnp.bfloat16

D_MODEL = 1024
CONV_WIDTH = 31
HGRN_HEADS = 8
HEAD_DIM = D_MODEL // HGRN_HEADS
N_EXPERTS = 64
N_GROUPS = 8
GROUP_SIZE = N_EXPERTS // N_GROUPS
TOPK_GROUPS = 4
TOP_K = 8
EXPERT_FF = 256
ROUTED_SCALE = 2.5
LN_EPS = 1e-5
LOG2_E = 1.4426950408889634
NEG_BIG = 1e30
SUBLANES = 8
LANES = 128
DEPTH = 1
DEEPNORM_ALPHA = (2.0 * DEPTH) ** 0.25

V7X_VMEM_BYTES = 64 * 1024 * 1024
VMEM_LIMIT = V7X_VMEM_BYTES - 8 * 1024 * 1024

TOKEN_BLOCK = 512
CONV_ROWS = 128
NORM_ROWS = 64
SAMPLE_SEQ_BLOCK = 32
CONV_HALO = 32
HGRN_BLOCK = 1024
HGRN_HEAD_GROUP = 8
HGRN_CHUNK = 64
HGRN_MILD_LOG2 = 96.0
HGRN_SUB = 8
HGRN_SAMPLE_BLOCK = 8
SAMPLE_PAD = 16
MOE_TOKEN_BLOCK = 1024
MOE_SUB_BLOCK = 512
MOE_GROUP_CAP = 288
MERGE_TOKEN_BLOCK = 1024
EXPERTS_PER_STEP = 4


def _sigmoid(x):
    return 1.0 / (1.0 + jnp.exp(-x))


def _silu(x):
    return x * _sigmoid(x)


def _layer_norm(x, g, b):
    mu = jnp.mean(x, axis=-1, keepdims=True)
    xc = x - mu
    var = jnp.mean(xc * xc, axis=-1, keepdims=True)
    return xc * lax.rsqrt(var + LN_EPS) * g + b


def _dot(a, b):
    return jnp.dot(a, b, preferred_element_type=F32)


def _dot_nt(a, b):
    return lax.dot_general(a, b, (((1,), (1,)), ((), ())), preferred_element_type=F32)


def _full(shape):
    return pl.BlockSpec(shape, lambda *_: (0,) * len(shape))


def _params(sem):
    return pltpu.CompilerParams(dimension_semantics=sem, vmem_limit_bytes=VMEM_LIMIT)


def _proj_columns(x_ref, w_ref, b_ref, lbp_ref, layer, emit_u, q_ref, k_ref, v_ref, lf_ref, sg_ref, ga_ref, gb_ref):
    D = D_MODEL
    xb = x_ref[...].astype(BF16)

    def col(j):
        return _dot(xb, w_ref[:, j * D:(j + 1) * D]) + b_ref[:, j * D:(j + 1) * D]

    emit_u(col(0) * _sigmoid(col(1)))
    hl = lbp_ref[...]
    e = jnp.exp(hl - jnp.max(hl, axis=0, keepdims=True))
    lb = jnp.sum(e[:layer + 1], axis=0, keepdims=True) / jnp.sum(e, axis=0, keepdims=True)
    fz = col(2)
    lf_ref[...] = jnp.log(lb + (1.0 - lb) * _sigmoid(fz))
    k_ref[...] = ((1.0 - lb) * _sigmoid(-fz)).astype(k_ref.dtype)
    v_ref[...] = col(3).astype(v_ref.dtype)
    q_ref[...] = _silu(col(4)).astype(q_ref.dtype)
    sg_ref[...] = _silu(col(5)).astype(sg_ref.dtype)
    ga_ref[...] = _sigmoid(col(6)).astype(ga_ref.dtype)
    gb_ref[...] = _sigmoid(col(7)).astype(gb_ref.dtype)


def _proj_kernel(x_ref, w_ref, b_ref, lbp_ref, u_ref, q_ref, k_ref, v_ref, lf_ref, sg_ref,
                 ga_ref, gb_ref, *, layer):
    def emit_u(u):
        u_ref[...] = u

    _proj_columns(x_ref, w_ref, b_ref, lbp_ref, layer, emit_u, q_ref, k_ref, v_ref, lf_ref, sg_ref, ga_ref, gb_ref)


def _proj(x2d, w_in, b_in, hgrn_lb, layer):
    n, d = x2d.shape
    tm = TOKEN_BLOCK
    cols = w_in.shape[1]
    row = pl.BlockSpec((tm, d), lambda i: (i, 0))
    outs = [jax.ShapeDtypeStruct((n, d), dt) for dt in (F32, BF16, BF16, BF16, F32, BF16, BF16, BF16)]
    return pl.pallas_call(
        functools.partial(_proj_kernel, layer=layer),
        grid=(n // tm,),
        in_specs=[row,
                  pl.BlockSpec((d, cols), lambda i: (0, 0), pipeline_mode=pl.Buffered(1)),
                  _full((1, cols)), _full(hgrn_lb.shape)],
        out_specs=[row] * 8,
        out_shape=outs,
        compiler_params=_params(("arbitrary",)),
        name="proj",
    )(x2d, w_in, b_in, hgrn_lb)


def _conv_tap_groups():
    shift = CONV_HALO - (CONV_WIDTH - 1)
    groups = [[] for _ in range(SUBLANES)]
    for j in range(CONV_WIDTH):
        groups[(j + shift) % SUBLANES].append((j, (j + shift) // SUBLANES))
    return groups


def _conv_rows(ext_ref, cw_ref, cb_ref, y_ref, base, rows):
    for l in range(D_MODEL // LANES):
        lanes = slice(l * LANES, (l + 1) * LANES)
        acc = None
        for res, taps in enumerate(_conv_tap_groups()):
            part = None
            for j, a in taps:
                term = cw_ref[j:j + 1, lanes] * ext_ref[pl.ds(base + SUBLANES * a, rows + SUBLANES), lanes]
                part = term if part is None else part + term
            part = part[res:res + rows, :]
            acc = part if acc is None else acc + part
        y_ref[pl.ds(base, rows), lanes] = acc + cb_ref[:, lanes]


def _proj_conv_kernel(x_ref, w_ref, b_ref, lbp_ref, cw_ref, cb_ref, g_ref, bb_ref,
                      h_ref, tail_ref, q_ref, k_ref, v_ref, lf_ref, sg_ref, ga_ref, gb_ref,
                      ext_ref, y_ref, *, layer):
    tm = x_ref.shape[0]

    @pl.when(pl.program_id(1) == 0)
    def _():
        ext_ref[0:CONV_HALO, :] = jnp.zeros((CONV_HALO, D_MODEL), F32)
        ext_ref[CONV_HALO + tm:CONV_HALO + tm + SUBLANES, :] = jnp.zeros((SUBLANES, D_MODEL), F32)

    def emit_u(u):
        ext_ref[CONV_HALO:CONV_HALO + tm, :] = u
        tail_ref[...] = u[tm - CONV_HALO:tm, :]
        for r in range(tm // CONV_ROWS):
            _conv_rows(ext_ref, cw_ref, cb_ref, y_ref, r * CONV_ROWS, CONV_ROWS)
        for r in range(tm // NORM_ROWS):
            rows = slice(r * NORM_ROWS, (r + 1) * NORM_ROWS)
            h_ref[rows, :] = _silu(_layer_norm(y_ref[rows, :], g_ref[...], bb_ref[...])).astype(h_ref.dtype)
        ext_ref[0:CONV_HALO, :] = ext_ref[tm:tm + CONV_HALO, :]

    _proj_columns(x_ref, w_ref, b_ref, lbp_ref, layer, emit_u, q_ref, k_ref, v_ref, lf_ref, sg_ref, ga_ref, gb_ref)


def _proj_conv(x2d, seq_len, w_in, b_in, hgrn_lb, layer, conv_w, conv_b, g, b):
    n, d = x2d.shape
    tm = TOKEN_BLOCK
    nt = seq_len // tm
    cols = w_in.shape[1]
    row = pl.BlockSpec((tm, d), lambda s, j: (s * nt + j, 0))
    outs = [jax.ShapeDtypeStruct((n, d), BF16), jax.ShapeDtypeStruct((n // seq_len * CONV_HALO, d), F32)]
    outs += [jax.ShapeDtypeStruct((n, d), dt) for dt in (BF16, BF16, BF16, F32, BF16, BF16, BF16)]
    return pl.pallas_call(
        functools.partial(_proj_conv_kernel, layer=layer),
        grid=(n // seq_len, nt),
        in_specs=[row,
                  pl.BlockSpec((d, cols), lambda s, j: (0, 0), pipeline_mode=pl.Buffered(1)),
                  _full((1, cols)), _full(hgrn_lb.shape),
                  _full(conv_w.shape), _full((1, d)), _full((1, d)), _full((1, d))],
        out_specs=[row, pl.BlockSpec((CONV_HALO, d), lambda s, j: (s, 0))] + [row] * 7,
        out_shape=outs,
        scratch_shapes=[pltpu.VMEM((CONV_HALO + tm + SUBLANES, d), F32), pltpu.VMEM((tm, d), F32)],
        compiler_params=_params(("arbitrary", "arbitrary")),
        name="proj_conv",
    )(x2d, w_in, b_in, hgrn_lb, conv_w, conv_b, g, b)


def _conv_sample_kernel(u_ref, st_ref, cw_ref, cb_ref, g_ref, b_ref, h_ref, new_ref):
    t_new, sb, d = u_ref.shape
    nbuf = st_ref.shape[1]

    def ext_row(r, rows):
        if r < nbuf:
            return st_ref[rows, r, :]
        return u_ref[r - nbuf, rows, :]

    def body(gi, carry):
        rows = pl.ds(pl.multiple_of(gi * SUBLANES, SUBLANES), SUBLANES)
        for t in range(t_new):
            acc = jnp.zeros((SUBLANES, d), F32) + cb_ref[...]
            for j in range(CONV_WIDTH):
                acc = acc + cw_ref[j:j + 1, :] * ext_row(t + j, rows)
            h_ref[t, rows, :] = _silu(_layer_norm(acc, g_ref[...], b_ref[...])).astype(h_ref.dtype)
        return carry

    lax.fori_loop(0, sb // SUBLANES, body, 0)
    new_ref[:, 0:nbuf - t_new, :] = st_ref[:, t_new:nbuf, :]
    for t in range(t_new):
        new_ref[:, nbuf - t_new + t, :] = u_ref[t]


def _conv_sample(u, state, conv_w, conv_b, g, b):
    t, bsz, d = u.shape
    nbuf = state.shape[1]
    sb = SAMPLE_SEQ_BLOCK
    return pl.pallas_call(
        _conv_sample_kernel,
        grid=(bsz // sb,),
        in_specs=[pl.BlockSpec((t, sb, d), lambda i: (0, i, 0)),
                  pl.BlockSpec((sb, nbuf, d), lambda i: (i, 0, 0)),
                  _full(conv_w.shape), _full((1, d)), _full((1, d)), _full((1, d))],
        out_specs=[pl.BlockSpec((t, sb, d), lambda i: (0, i, 0)),
                   pl.BlockSpec((sb, nbuf, d), lambda i: (i, 0, 0))],
        out_shape=[jax.ShapeDtypeStruct((t, bsz, d), F32),
                   jax.ShapeDtypeStruct((bsz, nbuf, d), F32)],
        compiler_params=_params(("arbitrary",)),
        name="conv_sample",
    )(u, state, conv_w, conv_b, g, b)


def _hgrn_chunk(q, k, v, g, st, consts, chunk, sub):
    _, blk_mask, neg_masks = consts
    nb = chunk // sub
    lcum = g
    l_end = lcum[chunk - 1:chunk, :]

    o = _dot_nt((q * jnp.exp2(lcum)).astype(BF16), st.astype(BF16))

    if nb > 1:
        starts = [lcum[i * sub - 1:i * sub, :] for i in range(1, nb)]
        l_start = jnp.concatenate(
            [jnp.zeros((sub, HEAD_DIM), F32)]
            + [jnp.broadcast_to(s, (sub, HEAD_DIM)) for s in starts], axis=0)
        q_rel = (q * jnp.exp2(lcum - l_start)).astype(BF16)
        k_stack = jnp.concatenate(
            [k[0:i * sub] * jnp.exp2(starts[i - 1] - lcum[0:i * sub]) for i in range(1, nb)],
            axis=0).astype(BF16)
        v_stack = jnp.concatenate([v[0:i * sub] for i in range(1, nb)], axis=0).astype(BF16)
        scores = _dot_nt(q_rel, k_stack) * blk_mask
        o = o + _dot(scores.astype(BF16), v_stack)

    diag = []
    for i in range(nb):
        sl = slice(i * sub, (i + 1) * sub)
        qi, ki, li = q[sl], k[sl], lcum[sl]
        od = None
        for s in range(sub):
            dec = jnp.exp2(li - li[s:s + 1, :] + neg_masks[s])
            score = jnp.sum(qi * (ki[s:s + 1, :] * dec), axis=-1, keepdims=True)
            term = score * v[i * sub + s:i * sub + s + 1, :]
            od = term if od is None else od + term
        diag.append(od)
    o = o + (jnp.concatenate(diag, axis=0) if nb > 1 else diag[0])

    k_end = (k * jnp.exp2(l_end - lcum)).astype(BF16)
    st_new = st * jnp.exp2(l_end) + _dot(v.T.astype(BF16), k_end)
    return o, st_new


def _hgrn_chunk_mild(q, k, v, g, st, causal):
    chunk = q.shape[0]
    l_end = g[chunk - 1:chunk, :]
    q_dec = (q * jnp.exp2(g)).astype(BF16)
    k_inv = (k * jnp.exp2(-g)).astype(BF16)
    scores = jnp.where(causal, _dot_nt(q_dec, k_inv), 0.0).astype(BF16)
    o = _dot(jnp.concatenate([q_dec, scores], axis=1),
             jnp.concatenate([st.T.astype(BF16), v.astype(BF16)], axis=0))
    st_new = (st + _dot(v.T.astype(BF16), k_inv)) * jnp.exp2(l_end)
    return o, st_new


def _hgrn_consts(chunk, sub):
    nb = chunk // sub
    r = lax.broadcasted_iota(jnp.int32, (chunk, chunk), 0)
    c = lax.broadcasted_iota(jnp.int32, (chunk, chunk), 1)
    tri = jnp.where(c <= r, 1.0, 0.0).astype(BF16)
    row = lax.broadcasted_iota(jnp.int32, (sub, HEAD_DIM), 0)
    neg_masks = [jnp.where(row >= s, 0.0, -NEG_BIG).astype(F32) for s in range(sub)]
    blk_mask = None
    if nb > 1:
        width = sub * nb * (nb - 1) // 2
        rb = lax.broadcasted_iota(jnp.int32, (chunk, width), 0) // sub
        cc = lax.broadcasted_iota(jnp.int32, (chunk, width), 1)
        blk_mask = jnp.zeros((chunk, width), F32)
        off = 0
        for i in range(1, nb):
            hit = jnp.where(rb == i, jnp.where(cc >= off, jnp.where(cc < off + i * sub, 1.0, 0.0), 0.0), 0.0)
            blk_mask = blk_mask + hit
            off += i * sub
    return tri, blk_mask, neg_masks


def _cumsum_rows(tri_bf, g):
    g1 = g.astype(BF16)
    r1 = g - g1.astype(F32)
    g2 = r1.astype(BF16)
    g3 = (r1 - g2.astype(F32)).astype(BF16)
    return _dot(tri_bf, g1) + _dot(tri_bf, g2) + _dot(tri_bf, g3)


def _head_norm(o, gn):
    return o * lax.rsqrt(jnp.mean(o * o, axis=-1, keepdims=True) + LN_EPS) * gn


def _hgrn_prompt_kernel(q_ref, k_ref, v_ref, lf_ref, sg_ref, gn_ref, s0_ref, o_ref, s_ref,
                        st_ref, lc_ref, mild_ref, *, chunk, sub):
    tb = q_ref.shape[1]
    n_chunks = tb // chunk
    heads = q_ref.shape[2] // HEAD_DIM
    j = pl.program_id(2)
    consts = _hgrn_consts(chunk, sub)
    gn = gn_ref[...]

    @pl.when(j == 0)
    def _():
        for h in range(heads):
            st_ref[h] = s0_ref[0, h].T

    causal = (lax.broadcasted_iota(jnp.int32, (chunk, chunk), 1)
              <= lax.broadcasted_iota(jnp.int32, (chunk, chunk), 0))

    def prepare(n, slot):
        rows = pl.ds(pl.multiple_of(n * chunk, chunk), chunk)
        lcum = _cumsum_rows(consts[0], lf_ref[0, rows, :]) * LOG2_E
        lc_ref[slot] = lcum
        mild_ref[slot] = (jnp.min(lcum[chunk - 1:chunk, :]) >= -HGRN_MILD_LOG2).astype(jnp.int32)

    prepare(0, 0)

    def body(n, carry):
        rows = pl.ds(pl.multiple_of(n * chunk, chunk), chunk)
        slot = lax.rem(n, 2)
        mild = mild_ref[slot] == 1

        def run(step):
            for h in range(heads):
                lanes = slice(h * HEAD_DIM, (h + 1) * HEAD_DIM)
                o, st = step(q_ref[0, rows, lanes].astype(F32), k_ref[0, rows, lanes].astype(F32),
                             v_ref[0, rows, lanes].astype(F32), lc_ref[slot, :, lanes], st_ref[h])
                st_ref[h] = st
                o_ref[0, rows, lanes] = (_head_norm(o, gn) * sg_ref[0, rows, lanes].astype(F32)).astype(o_ref.dtype)
            prepare(jnp.minimum(n + 1, n_chunks - 1), 1 - slot)

        @pl.when(mild)
        def _():
            run(lambda q, k, v, g, st: _hgrn_chunk_mild(q, k, v, g, st, causal))

        @pl.when(jnp.logical_not(mild))
        def _():
            run(lambda q, k, v, g, st: _hgrn_chunk(q, k, v, g, st, consts, chunk, sub))

        return carry

    lax.fori_loop(0, n_chunks, body, 0)

    @pl.when(j == pl.num_programs(2) - 1)
    def _():
        for h in range(heads):
            s_ref[0, h] = st_ref[h].T


def _hgrn_prompt(q, k, v, lf, sg, gnorm, s0):
    bsz, t, d = q.shape
    tb = HGRN_BLOCK
    hg = HGRN_HEAD_GROUP
    blk = pl.BlockSpec((1, tb, hg * HEAD_DIM), lambda b, g, j: (b, j, g))
    sblk = pl.BlockSpec((1, hg, HEAD_DIM, HEAD_DIM), lambda b, g, j: (b, g, 0, 0))
    return pl.pallas_call(
        functools.partial(_hgrn_prompt_kernel, chunk=HGRN_CHUNK, sub=HGRN_SUB),
        grid=(bsz, HGRN_HEADS // hg, t // tb),
        in_specs=[blk, blk, blk, blk, blk, _full((1, HEAD_DIM)), sblk],
        out_specs=[blk, sblk],
        out_shape=[jax.ShapeDtypeStruct((bsz, t, d), BF16),
                   jax.ShapeDtypeStruct(s0.shape, F32)],
        scratch_shapes=[pltpu.VMEM((hg, HEAD_DIM, HEAD_DIM), F32),
                        pltpu.VMEM((2, HGRN_CHUNK, hg * HEAD_DIM), F32), pltpu.SMEM((2,), jnp.int32)],
        compiler_params=_params(("arbitrary", "arbitrary", "arbitrary")),
        name="hgrn_prompt",
    )(q, k, v, lf, sg, gnorm, s0)


def _hgrn_sample_kernel(q_ref, k_ref, v_ref, lf_ref, sg_ref, gn_ref, s0_ref, o_ref, s_ref):
    sb, t_new, d_model = q_ref.shape
    gn = gn_ref[...]
    pad_rows = SAMPLE_PAD - t_new - 1

    def body(b, carry):
        q, k, v, sg = q_ref[b], k_ref[b], v_ref[b], sg_ref[b]
        lf = lf_ref[b]
        lrow = []
        for t in range(t_new):
            lrow.append(lf[t:t + 1, :] if t == 0 else lrow[-1] + lf[t:t + 1, :])
        lcum = jnp.concatenate(lrow, axis=0)
        l_end = lrow[-1]
        qe = q * jnp.exp(lcum)
        ke = k * jnp.exp(l_end - lcum)
        f_end = jnp.exp(l_end)
        zeros = jnp.zeros((pad_rows, d_model), F32)
        zero_row = jnp.zeros((1, d_model), F32)
        qe_p = jnp.concatenate([qe, zero_row, zeros], axis=0)
        ke_p = jnp.concatenate([ke, f_end, zeros], axis=0)
        v_p = jnp.concatenate([v, zero_row, zeros], axis=0)
        for h in range(HGRN_HEADS):
            lanes = slice(h * HEAD_DIM, (h + 1) * HEAD_DIM)
            s0 = s0_ref[b, h]
            o = _dot(qe_p[:, lanes].astype(BF16), s0.astype(BF16))[0:t_new]
            rows = []
            for t in range(t_new):
                ot = None
                for s in range(t + 1):
                    prod = q[t:t + 1, lanes] * k[s:s + 1, lanes]
                    if s < t:
                        prod = prod * jnp.exp(lrow[t][:, lanes] - lrow[s][:, lanes])
                    term = jnp.sum(prod, axis=-1, keepdims=True) * v[s:s + 1, lanes]
                    ot = term if ot is None else ot + term
                rows.append(ot)
            o = o + jnp.concatenate(rows, axis=0)
            ke_t = ke_p[:, lanes].T
            decay = ke_t[:, t_new:t_new + 1]
            s_ref[b, h] = s0 * decay + _dot(ke_t.astype(BF16), v_p[:, lanes].astype(BF16))
            o_ref[b, :, lanes] = _head_norm(o, gn) * sg[:, lanes]
        return carry

    lax.fori_loop(0, sb, body, 0)


def _hgrn_sample(q, k, v, lf, sg, gnorm, s0):
    bsz, t, d = q.shape
    sb = HGRN_SAMPLE_BLOCK
    blk = pl.BlockSpec((sb, t, d), lambda g: (g, 0, 0))
    sblk = pl.BlockSpec((sb, HGRN_HEADS, HEAD_DIM, HEAD_DIM), lambda g: (g, 0, 0, 0))
    return pl.pallas_call(
        _hgrn_sample_kernel,
        grid=(bsz // sb,),
        in_specs=[blk, blk, blk, blk, blk, _full((1, HEAD_DIM)), sblk],
        out_specs=[blk, sblk],
        out_shape=[jax.ShapeDtypeStruct((bsz, t, d), F32),
                   jax.ShapeDtypeStruct(s0.shape, F32)],
        compiler_params=_params(("arbitrary",)),
        name="hgrn_sample",
    )(q, k, v, lf, sg, gnorm, s0)


def _merge_kernel(x_ref, ha_ref, hb_ref, ga_ref, gb_ref, wa_ref, wb_ref, wo_ref, g_ref, b_ref, o_ref):
    ya = _dot(ha_ref[...].astype(BF16), wa_ref[...])
    yb = _dot(hb_ref[...].astype(BF16), wb_ref[...])
    mixed = ga_ref[...].astype(F32) * ya + gb_ref[...].astype(F32) * yb
    z = DEEPNORM_ALPHA * x_ref[...] + _dot(mixed.astype(BF16), wo_ref[...])
    o_ref[...] = _layer_norm(z, g_ref[...], b_ref[...])


def _merge(x2d, ha, hb, ga, gb, wa, wb, wo, g, b):
    n, d = x2d.shape
    tm = MERGE_TOKEN_BLOCK if n % MERGE_TOKEN_BLOCK == 0 else TOKEN_BLOCK
    row = pl.BlockSpec((tm, d), lambda i: (i, 0))
    return pl.pallas_call(
        _merge_kernel,
        grid=(n // tm,),
        in_specs=[row] * 5 + [_full((d, d))] * 3 + [_full((1, d))] * 2,
        out_specs=row,
        out_shape=jax.ShapeDtypeStruct((n, d), F32),
        compiler_params=_params(("arbitrary",)),
        name="merge",
    )(x2d, ha, hb, ga, gb, wa, wb, wo, g, b)


def _route_t(s, bias):
    n = s.shape[1]
    neg = -jnp.inf
    sb = (s + bias).reshape(GROUP_SIZE, N_GROUPS, n)
    s3 = s.reshape(GROUP_SIZE, N_GROUPS, n)
    e_in_g = lax.broadcasted_iota(jnp.int32, sb.shape, 0)
    m1 = jnp.max(sb, axis=0, keepdims=True)
    first = jnp.min(jnp.where(sb == m1, e_in_g, GROUP_SIZE), axis=0, keepdims=True)
    m2 = jnp.max(jnp.where(e_in_g == first, neg, sb), axis=0, keepdims=True)
    gscore = (m1 + m2)[0]
    gid = lax.broadcasted_iota(jnp.int32, gscore.shape, 0)
    gsel = jnp.zeros(gscore.shape, F32)
    for _ in range(TOPK_GROUPS):
        gm = jnp.max(gscore, axis=0, keepdims=True)
        pick = jnp.min(jnp.where(gscore == gm, gid, N_GROUPS), axis=0, keepdims=True)
        hit = gid == pick
        gsel = jnp.where(hit, 1.0, gsel)
        gscore = jnp.where(hit, neg, gscore)
    cand = jnp.where(gsel[None, :, :] > 0.5, sb, neg)
    eid = lax.broadcasted_iota(jnp.int32, sb.shape, 1) * GROUP_SIZE + e_in_g
    esel = jnp.zeros(sb.shape, F32)
    for _ in range(TOP_K):
        em = jnp.max(jnp.max(cand, axis=0, keepdims=True), axis=1, keepdims=True)
        masked = jnp.where(cand == em, eid, N_EXPERTS)
        pick = jnp.min(jnp.min(masked, axis=0, keepdims=True), axis=1, keepdims=True)
        hit = eid == pick
        esel = jnp.where(hit, 1.0, esel)
        cand = jnp.where(hit, neg, cand)
    w = esel * s3
    tot = jnp.sum(jnp.sum(w, axis=0, keepdims=True), axis=1, keepdims=True)
    return w / tot * ROUTED_SCALE, jnp.max(esel, axis=0)


def _split_bf16(a):
    hi = a.astype(BF16)
    return hi, (a - hi.astype(F32)).astype(BF16)


def _moe_kernel(x_ref, p_ref, wrt_ref, rb_ref, wg_ref, wu_ref, wd_ref, sg_ref, su_ref, sd_ref,
                g2_ref, b2_ref, pg_ref, pp_ref, o_ref,
                xb_ref, gt_ref, acc_ref, key_ref, ok_ref, xg_ref, gg_ref, pt_ref, accg_ref):
    j = pl.program_id(1)
    eb = wg_ref.shape[0]
    tm = x_ref.shape[0]
    n_sub, cap = xg_ref.shape[0], xg_ref.shape[1]
    ts = tm // n_sub
    steps_per_group = GROUP_SIZE // eb
    group = j // steps_per_group
    step_in_group = j - group * steps_per_group

    @pl.when(j == 0)
    def _():
        x = x_ref[...]
        xb = x.astype(BF16)
        x_lo = (x - xb.astype(F32)).astype(BF16)
        w_hi, w_lo = _split_bf16(wrt_ref[...])
        logits = _dot_nt(w_hi, xb) + _dot_nt(w_hi, x_lo) + _dot_nt(w_lo, xb)
        gates, gsel = _route_t(_sigmoid(logits), rb_ref[...])
        for i in range(GROUP_SIZE):
            for g in range(N_GROUPS):
                gt_ref[g * GROUP_SIZE + i:g * GROUP_SIZE + i + 1, :] = gates[i, g:g + 1, :]
        earlier = (lax.broadcasted_iota(jnp.int32, (ts, ts), 0)
                   < lax.broadcasted_iota(jnp.int32, (ts, ts), 1))
        before = jnp.where(earlier, 1.0, 0.0).astype(BF16)
        for s in range(n_sub):
            sel = gsel[:, s * ts:(s + 1) * ts]
            key = jnp.where(sel > 0.5, _dot(sel.astype(BF16), before), -1.0)
            for g in range(N_GROUPS):
                key_ref[s, g] = key[g:g + 1, :]
        xb_ref[...] = xb
        hs = _silu(_dot(xb, sg_ref[...])) * _dot(xb, su_ref[...])
        acc_ref[...] = _dot(hs.astype(BF16), sd_ref[...])

    slab = pl.multiple_of(group * GROUP_SIZE, GROUP_SIZE)
    in_group = step_in_group * eb

    def experts(rows, gate_cols, add):
        for r in range(eb):
            gate = gate_cols[:, r:r + 1]
            for off in range(eb, GROUP_SIZE, eb):
                gate = jnp.where(in_group == off, gate_cols[:, off + r:off + r + 1], gate)
            h = _silu(_dot(rows, wg_ref[r])) * _dot(rows, wu_ref[r])
            add(_dot((h * gate).astype(BF16), wd_ref[r]))

    def add_compact(s):
        def add(y):
            accg_ref[s] += y
        return add

    def add_dense(tok):
        def add(y):
            acc_ref[tok, :] += y
        return add

    def add_all_compact(y):
        accg_ref[...] += y.reshape(accg_ref.shape)

    for s in range(n_sub):
        tok = slice(s * ts, (s + 1) * ts)

        @pl.when(step_in_group == 0)
        def _():
            key_row = key_ref[s, group]
            count = jnp.sum(jnp.where(key_row >= 0.0, 1.0, 0.0))
            fits = count <= float(cap)
            ok_ref[s] = fits.astype(jnp.int32)

            @pl.when(fits)
            def _():
                slot = lax.broadcasted_iota(jnp.int32, (cap, ts), 0).astype(F32)
                pick = jnp.where(slot == key_row, 1.0, 0.0).astype(BF16)
                xg_ref[s] = _dot(pick, xb_ref[tok, :]).astype(BF16)
                g_rows = jnp.concatenate([gt_ref[pl.ds(slab, GROUP_SIZE), tok],
                                          jnp.zeros((LANES - GROUP_SIZE, ts), F32)], axis=0)
                g_hi, g_lo = _split_bf16(g_rows.T)
                gg_ref[s] = _dot(pick, g_hi) + _dot(pick, g_lo)
                key_col = jnp.broadcast_to(key_row, (SUBLANES, ts)).T[:, 0:1]
                slot_t = lax.broadcasted_iota(jnp.int32, (ts, cap), 1).astype(F32)
                pt_ref[s] = jnp.where(slot_t == key_col, 1.0, 0.0).astype(BF16)
                accg_ref[s] = jnp.zeros((cap, accg_ref.shape[2]), F32)

    flags = [ok_ref[s] == 1 for s in range(n_sub)]
    all_compact = functools.reduce(jnp.logical_and, flags)

    @pl.when(all_compact)
    def _():
        experts(xg_ref[...].reshape(n_sub * cap, xg_ref.shape[2]),
                gg_ref[...].reshape(n_sub * cap, LANES), add_all_compact)

    for s in range(n_sub):
        tok = slice(s * ts, (s + 1) * ts)
        alone = jnp.logical_and(flags[s], jnp.logical_not(all_compact))

        @pl.when(alone)
        def _():
            experts(xg_ref[s], gg_ref[s], add_compact(s))

        @pl.when(jnp.logical_not(flags[s]))
        def _():
            experts(xb_ref[tok, :], gt_ref[pl.ds(slab, GROUP_SIZE), tok].T, add_dense(tok))

        @pl.when(jnp.logical_and(flags[s], step_in_group == steps_per_group - 1))
        def _():
            acc_ref[tok, :] += _dot(pt_ref[s], accg_ref[s].astype(BF16))

    @pl.when(j == pl.num_programs(1) - 1)
    def _():
        x2 = _layer_norm(DEEPNORM_ALPHA * x_ref[...] + acc_ref[...], g2_ref[...], b2_ref[...])
        gate = _sigmoid(_dot(x2.astype(BF16), pg_ref[...]))
        o_ref[...] = x2 + gate * _dot(p_ref[...].astype(BF16), pp_ref[...])


def _moe(x2d, p2d, wrt, rbias, wg, wu, wd, sg, su, sd, g2, b2, pg, pp):
    n, d = x2d.shape
    tm = MOE_TOKEN_BLOCK if n % MOE_TOKEN_BLOCK == 0 else MOE_SUB_BLOCK
    ts = MOE_SUB_BLOCK
    n_sub = tm // ts
    eb = EXPERTS_PER_STEP
    cap = MOE_GROUP_CAP
    assert GROUP_SIZE % eb == 0 and n % tm == 0 and tm % ts == 0
    dp = p2d.shape[1]
    ff = wg.shape[2]
    sff = sg.shape[1]
    row = pl.BlockSpec((tm, d), lambda i, j: (i, 0))
    return pl.pallas_call(
        _moe_kernel,
        grid=(n // tm, N_EXPERTS // eb),
        in_specs=[row, pl.BlockSpec((tm, dp), lambda i, j: (i, 0)),
                  _full((N_EXPERTS, d)), _full((N_EXPERTS, 1)),
                  pl.BlockSpec((eb, d, ff), lambda i, j: (j, 0, 0)),
                  pl.BlockSpec((eb, d, ff), lambda i, j: (j, 0, 0)),
                  pl.BlockSpec((eb, ff, d), lambda i, j: (j, 0, 0)),
                  _full((d, sff)), _full((d, sff)), _full((sff, d)),
                  _full((1, d)), _full((1, d)), _full((d, d)), _full((dp, d))],
        out_specs=row,
        out_shape=jax.ShapeDtypeStruct((n, d), F32),
        scratch_shapes=[pltpu.VMEM((tm, d), BF16), pltpu.VMEM((N_EXPERTS, tm), F32),
                        pltpu.VMEM((tm, d), F32),
                        pltpu.VMEM((n_sub, N_GROUPS, 1, ts), F32), pltpu.SMEM((n_sub,), jnp.int32),
                        pltpu.VMEM((n_sub, cap, d), BF16), pltpu.VMEM((n_sub, cap, LANES), F32),
                        pltpu.VMEM((n_sub, ts, cap), BF16), pltpu.VMEM((n_sub, cap, d), F32)],
        compiler_params=_params(("arbitrary", "arbitrary")),
        name="moe",
    )(x2d, p2d, wrt, rbias, wg, wu, wd, sg, su, sd, g2, b2, pg, pp)


def kernel(x_prompt, x_sample, p_prompt, p_sample, state_conv, state_hgrn, w_in, b_in, hgrn_lb, conv_w, conv_b, conv_ln_g, conv_ln_b, w_conv_out, hgrn_norm_g, w_hgrn_out, w_o, ln1_g, ln1_b, w_router, router_bias, w_exp_gate, w_exp_up, w_exp_down, w_sh_gate, w_sh_up, w_sh_down, ln2_g, ln2_b, w_ple_gate, w_ple_proj):
    assert w_in.shape[0] == DEPTH == 1
    bp, tp, d = x_prompt.shape
    bs, ts, _ = x_sample.shape
    nbuf = CONV_WIDTH - 1
    i = 0
    layer = lambda a: a.reshape(a.shape[1:])
    row = lambda a: a.reshape(1, -1)
    bf = lambda a: layer(a).astype(BF16)

    w_in_b, wco, who, wo = bf(w_in), bf(w_conv_out), bf(w_hgrn_out), bf(w_o)
    weg, weu, wed = bf(w_exp_gate), bf(w_exp_up), bf(w_exp_down)
    wsg, wsu, wsd = bf(w_sh_gate), bf(w_sh_up), bf(w_sh_down)
    wpg, wpp = bf(w_ple_gate), bf(w_ple_proj)
    regroup = lambda a: a.reshape(N_GROUPS, GROUP_SIZE, -1).swapaxes(0, 1).reshape(N_EXPERTS, -1)
    wrt = regroup(layer(w_router).T)
    rbias = regroup(router_bias.reshape(N_EXPERTS, 1))
    gnorm = row(hgrn_norm_g)

    def tail(x2d, p2d, ha, hb, ga, gb):
        x1 = _merge(x2d, ha, hb, ga, gb, wco, who, wo, row(ln1_g), row(ln1_b))
        return _moe(x1, p2d, wrt, rbias, weg, weu, wed, wsg, wsu, wsd, row(ln2_g), row(ln2_b), wpg, wpp)

    xp = x_prompt.reshape(bp * tp, d)
    ha, u_tail, q, k, v, lf, sg, ga, gb = _proj_conv(xp, tp, w_in_b, row(b_in), hgrn_lb, i, layer(conv_w),
                                                     row(conv_b), row(conv_ln_g), row(conv_ln_b))
    seq = lambda a: a.reshape(bp, tp, d)
    hb, hgrn_p = _hgrn_prompt(seq(q), seq(k), seq(v), seq(lf), seq(sg), gnorm,
                              jnp.zeros((bp, HGRN_HEADS, HEAD_DIM, HEAD_DIM), F32))
    y_p = tail(xp, p_prompt.reshape(bp * tp, -1), ha, hb.reshape(bp * tp, d), ga, gb)
    conv_p = u_tail.reshape(bp, CONV_HALO, d)[:, CONV_HALO - nbuf:, :]

    tmaj = lambda a: jnp.swapaxes(a, 0, 1).reshape(ts * bs, -1)
    xs = tmaj(x_sample)
    u, q, k, v, lf, sg, ga, gb = _proj(xs, w_in_b, row(b_in), hgrn_lb, i)
    seq = lambda a: a.reshape(ts, bs, d)
    ha, conv_s = _conv_sample(seq(u), layer(state_conv), layer(conv_w), row(conv_b), row(conv_ln_g),
                              row(conv_ln_b))
    bmaj = lambda a: jnp.swapaxes(seq(a), 0, 1).astype(F32)
    hb, hgrn_s = _hgrn_sample(bmaj(q), bmaj(k), bmaj(v), bmaj(lf), bmaj(sg), gnorm, layer(state_hgrn))
    y_s = tail(xs, tmaj(layer(p_sample)), ha.reshape(ts * bs, d), tmaj(hb), ga, gb)
    y_s = jnp.swapaxes(y_s.reshape(ts, bs, d), 0, 1)

    return (y_p.reshape(bp, tp, d), y_s, conv_p[None], hgrn_p[None], conv_s[None], hgrn_s[None])
```

```python
import functools

import jax
import jax.numpy as jnp
from jax import lax
from jax.experimental import pallas as pl
from jax.experimental.pallas import tpu as pltpu

F32 = jnp.float32
BF16 = jnp.bfloat16

D_MODEL = 1024
CONV_WIDTH = 31
HGRN_HEADS = 8
HEAD_DIM = D_MODEL // HGRN_HEADS
N_EXPERTS = 64
N_GROUPS = 8
GROUP_SIZE = N_EXPERTS // N_GROUPS
TOPK_GROUPS = 4
TOP_K = 8
EXPERT_FF = 256
ROUTED_SCALE = 2.5
LN_EPS = 1e-5
LOG2_E = 1.4426950408889634
NEG_BIG = 1e30
SUBLANES = 8
LANES = 128
DEPTH = 1
DEEPNORM_ALPHA = (2.0 * DEPTH) ** 0.25

V7X_VMEM_BYTES = 64 * 1024 * 1024
VMEM_LIMIT = V7X_VMEM_BYTES - 8 * 1024 * 1024

TOKEN_BLOCK = 512
CONV_ROWS = 128
NORM_ROWS = 64
SAMPLE_SEQ_BLOCK = 32
CONV_HALO = 32
HGRN_BLOCK = 1024
HGRN_HEAD_GROUP = 8
HGRN_CHUNK = 64
HGRN_MILD_LOG2 = 96.0
HGRN_SUB = 8
HGRN_SAMPLE_BLOCK = 8
SAMPLE_PAD = 16
MOE_TOKEN_BLOCK = 1024
MOE_SUB_BLOCK = 512
MOE_GROUP_CAP = 288
MERGE_TOKEN_BLOCK = 1024
EXPERTS_PER_STEP = 4


def _sigmoid(x):
    return 1.0 / (1.0 + jnp.exp(-x))


def _silu(x):
    return x * _sigmoid(x)


def _layer_norm(x, g, b):
    mu = jnp.mean(x, axis=-1, keepdims=True)
    xc = x - mu
    var = jnp.mean(xc * xc, axis=-1, keepdims=True)
    return xc * lax.rsqrt(var + LN_EPS) * g + b


def _dot(a, b):
    return jnp.dot(a, b, preferred_element_type=F32)


def _dot_nt(a, b):
    return lax.dot_general(a, b, (((1,), (1,)), ((), ())), preferred_element_type=F32)


def _full(shape):
    return pl.BlockSpec(shape, lambda *_: (0,) * len(shape))


def _params(sem):
    return pltpu.CompilerParams(dimension_semantics=sem, vmem_limit_bytes=VMEM_LIMIT)


def _proj_columns(x_ref, w_ref, b_ref, lbp_ref, layer, emit_u, q_ref, k_ref, v_ref, lf_ref, sg_ref, ga_ref, gb_ref):
    D = D_MODEL
    xb = x_ref[...].astype(BF16)

    def col(j):
        return _dot(xb, w_ref[:, j * D:(j + 1) * D]) + b_ref[:, j * D:(j + 1) * D]

    emit_u(col(0) * _sigmoid(col(1)))
    hl = lbp_ref[...]
    e = jnp.exp(hl - jnp.max(hl, axis=0, keepdims=True))
    lb = jnp.sum(e[:layer + 1], axis=0, keepdims=True) / jnp.sum(e, axis=0, keepdims=True)
    fz = col(2)
    lf_ref[...] = jnp.log(lb + (1.0 - lb) * _sigmoid(fz))
    k_ref[...] = ((1.0 - lb) * _sigmoid(-fz)).astype(k_ref.dtype)
    v_ref[...] = col(3).astype(v_ref.dtype)
    q_ref[...] = _silu(col(4)).astype(q_ref.dtype)
    sg_ref[...] = _silu(col(5)).astype(sg_ref.dtype)
    ga_ref[...] = _sigmoid(col(6)).astype(ga_ref.dtype)
    gb_ref[...] = _sigmoid(col(7)).astype(gb_ref.dtype)


def _proj_kernel(x_ref, w_ref, b_ref, lbp_ref, u_ref, q_ref, k_ref, v_ref, lf_ref, sg_ref,
                 ga_ref, gb_ref, *, layer):
    def emit_u(u):
        u_ref[...] = u

    _proj_columns(x_ref, w_ref, b_ref, lbp_ref, layer, emit_u, q_ref, k_ref, v_ref, lf_ref, sg_ref, ga_ref, gb_ref)


def _proj(x2d, w_in, b_in, hgrn_lb, layer):
    n, d = x2d.shape
    tm = TOKEN_BLOCK
    cols = w_in.shape[1]
    row = pl.BlockSpec((tm, d), lambda i: (i, 0))
    outs = [jax.ShapeDtypeStruct((n, d), dt) for dt in (F32, BF16, BF16, BF16, F32, BF16, BF16, BF16)]
    return pl.pallas_call(
        functools.partial(_proj_kernel, layer=layer),
        grid=(n // tm,),
        in_specs=[row,
                  pl.BlockSpec((d, cols), lambda i: (0, 0), pipeline_mode=pl.Buffered(1)),
                  _full((1, cols)), _full(hgrn_lb.shape)],
        out_specs=[row] * 8,
        out_shape=outs,
        compiler_params=_params(("arbitrary",)),
        name="proj",
    )(x2d, w_in, b_in, hgrn_lb)


def _conv_tap_groups():
    shift = CONV_HALO - (CONV_WIDTH - 1)
    groups = [[] for _ in range(SUBLANES)]
    for j in range(CONV_WIDTH):
        groups[(j + shift) % SUBLANES].append((j, (j + shift) // SUBLANES))
    return groups


def _conv_rows(ext_ref, cw_ref, cb_ref, y_ref, base, rows):
    for l in range(D_MODEL // LANES):
        lanes = slice(l * LANES, (l + 1) * LANES)
        acc = None
        for res, taps in enumerate(_conv_tap_groups()):
            part = None
            for j, a in taps:
                term = cw_ref[j:j + 1, lanes] * ext_ref[pl.ds(base + SUBLANES * a, rows + SUBLANES), lanes]
                part = term if part is None else part + term
            part = part[res:res + rows, :]
            acc = part if acc is None else acc + part
        y_ref[pl.ds(base, rows), lanes] = acc + cb_ref[:, lanes]


def _proj_conv_kernel(x_ref, w_ref, b_ref, lbp_ref, cw_ref, cb_ref, g_ref, bb_ref,
                      h_ref, tail_ref, q_ref, k_ref, v_ref, lf_ref, sg_ref, ga_ref, gb_ref,
                      ext_ref, y_ref, *, layer):
    tm = x_ref.shape[0]

    @pl.when(pl.program_id(1) == 0)
    def _():
        ext_ref[0:CONV_HALO, :] = jnp.zeros((CONV_HALO, D_MODEL), F32)
        ext_ref[CONV_HALO + tm:CONV_HALO + tm + SUBLANES, :] = jnp.zeros((SUBLANES, D_MODEL), F32)

    def emit_u(u):
        ext_ref[CONV_HALO:CONV_HALO + tm, :] = u
        tail_ref[...] = u[tm - CONV_HALO:tm, :]
        for r in range(tm // CONV_ROWS):
            _conv_rows(ext_ref, cw_ref, cb_ref, y_ref, r * CONV_ROWS, CONV_ROWS)
        for r in range(tm // NORM_ROWS):
            rows = slice(r * NORM_ROWS, (r + 1) * NORM_ROWS)
            h_ref[rows, :] = _silu(_layer_norm(y_ref[rows, :], g_ref[...], bb_ref[...])).astype(h_ref.dtype)
        ext_ref[0:CONV_HALO, :] = ext_ref[tm:tm + CONV_HALO, :]

    _proj_columns(x_ref, w_ref, b_ref, lbp_ref, layer, emit_u, q_ref, k_ref, v_ref, lf_ref, sg_ref, ga_ref, gb_ref)


def _proj_conv(x2d, seq_len, w_in, b_in, hgrn_lb, layer, conv_w, conv_b, g, b):
    n, d = x2d.shape
    tm = TOKEN_BLOCK
    nt = seq_len // tm
    cols = w_in.shape[1]
    row = pl.BlockSpec((tm, d), lambda s, j: (s * nt + j, 0))
    outs = [jax.ShapeDtypeStruct((n, d), BF16), jax.ShapeDtypeStruct((n // seq_len * CONV_HALO, d), F32)]
    outs += [jax.ShapeDtypeStruct((n, d), dt) for dt in (BF16, BF16, BF16, F32, BF16, BF16, BF16)]
    return pl.pallas_call(
        functools.partial(_proj_conv_kernel, layer=layer),
        grid=(n // seq_len, nt),
        in_specs=[row,
                  pl.BlockSpec((d, cols), lambda s, j: (0, 0), pipeline_mode=pl.Buffered(1)),
                  _full((1, cols)), _full(hgrn_lb.shape),
                  _full(conv_w.shape), _full((1, d)), _full((1, d)), _full((1, d))],
        out_specs=[row, pl.BlockSpec((CONV_HALO, d), lambda s, j: (s, 0))] + [row] * 7,
        out_shape=outs,
        scratch_shapes=[pltpu.VMEM((CONV_HALO + tm + SUBLANES, d), F32), pltpu.VMEM((tm, d), F32)],
        compiler_params=_params(("arbitrary", "arbitrary")),
        name="proj_conv",
    )(x2d, w_in, b_in, hgrn_lb, conv_w, conv_b, g, b)


def _conv_sample_kernel(u_ref, st_ref, cw_ref, cb_ref, g_ref, b_ref, h_ref, new_ref):
    t_new, sb, d = u_ref.shape
    nbuf = st_ref.shape[1]

    def ext_row(r, rows):
        if r < nbuf:
            return st_ref[rows, r, :]
        return u_ref[r - nbuf, rows, :]

    def body(gi, carry):
        rows = pl.ds(pl.multiple_of(gi * SUBLANES, SUBLANES), SUBLANES)
        for t in range(t_new):
            acc = jnp.zeros((SUBLANES, d), F32) + cb_ref[...]
            for j in range(CONV_WIDTH):
                acc = acc + cw_ref[j:j + 1, :] * ext_row(t + j, rows)
            h_ref[t, rows, :] = _silu(_layer_norm(acc, g_ref[...], b_ref[...])).astype(h_ref.dtype)
        return carry

    lax.fori_loop(0, sb // SUBLANES, body, 0)
    new_ref[:, 0:nbuf - t_new, :] = st_ref[:, t_new:nbuf, :]
    for t in range(t_new):
        new_ref[:, nbuf - t_new + t, :] = u_ref[t]


def _conv_sample(u, state, conv_w, conv_b, g, b):
    t, bsz, d = u.shape
    nbuf = state.shape[1]
    sb = SAMPLE_SEQ_BLOCK
    return pl.pallas_call(
        _conv_sample_kernel,
        grid=(bsz // sb,),
        in_specs=[pl.BlockSpec((t, sb, d), lambda i: (0, i, 0)),
                  pl.BlockSpec((sb, nbuf, d), lambda i: (i, 0, 0)),
                  _full(conv_w.shape), _full((1, d)), _full((1, d)), _full((1, d))],
        out_specs=[pl.BlockSpec((t, sb, d), lambda i: (0, i, 0)),
                   pl.BlockSpec((sb, nbuf, d), lambda i: (i, 0, 0))],
        out_shape=[jax.ShapeDtypeStruct((t, bsz, d), F32),
                   jax.ShapeDtypeStruct((bsz, nbuf, d), F32)],
        compiler_params=_params(("arbitrary",)),
        name="conv_sample",
    )(u, state, conv_w, conv_b, g, b)


def _hgrn_chunk(q, k, v, g, st, consts, chunk, sub):
    _, blk_mask, neg_masks = consts
    nb = chunk // sub
    lcum = g
    l_end = lcum[chunk - 1:chunk, :]

    o = _dot_nt((q * jnp.exp2(lcum)).astype(BF16), st.astype(BF16))

    if nb > 1:
        starts = [lcum[i * sub - 1:i * sub, :] for i in range(1, nb)]
        l_start = jnp.concatenate(
            [jnp.zeros((sub, HEAD_DIM), F32)]
            + [jnp.broadcast_to(s, (sub, HEAD_DIM)) for s in starts], axis=0)
        q_rel = (q * jnp.exp2(lcum - l_start)).astype(BF16)
        k_stack = jnp.concatenate(
            [k[0:i * sub] * jnp.exp2(starts[i - 1] - lcum[0:i * sub]) for i in range(1, nb)],
            axis=0).astype(BF16)
        v_stack = jnp.concatenate([v[0:i * sub] for i in range(1, nb)], axis=0).astype(BF16)
        scores = _dot_nt(q_rel, k_stack) * blk_mask
        o = o + _dot(scores.astype(BF16), v_stack)

    diag = []
    for i in range(nb):
        sl = slice(i * sub, (i + 1) * sub)
        qi, ki, li = q[sl], k[sl], lcum[sl]
        od = None
        for s in range(sub):
            dec = jnp.exp2(li - li[s:s + 1, :] + neg_masks[s])
            score = jnp.sum(qi * (ki[s:s + 1, :] * dec), axis=-1, keepdims=True)
            term = score * v[i * sub + s:i * sub + s + 1, :]
            od = term if od is None else od + term
        diag.append(od)
    o = o + (jnp.concatenate(diag, axis=0) if nb > 1 else diag[0])

    k_end = (k * jnp.exp2(l_end - lcum)).astype(BF16)
    st_new = st * jnp.exp2(l_end) + _dot(v.T.astype(BF16), k_end)
    return o, st_new


def _hgrn_chunk_mild(q, k, v, g, st, causal):
    chunk = q.shape[0]
    l_end = g[chunk - 1:chunk, :]
    q_dec = (q * jnp.exp2(g)).astype(BF16)
    k_inv = (k * jnp.exp2(-g)).astype(BF16)
    scores = jnp.where(causal, _dot_nt(q_dec, k_inv), 0.0).astype(BF16)
    o = _dot(jnp.concatenate([q_dec, scores], axis=1),
             jnp.concatenate([st.T.astype(BF16), v.astype(BF16)], axis=0))
    st_new = (st + _dot(v.T.astype(BF16), k_inv)) * jnp.exp2(l_end)
    return o, st_new


def _hgrn_consts(chunk, sub):
    nb = chunk // sub
    r = lax.broadcasted_iota(jnp.int32, (chunk, chunk), 0)
    c = lax.broadcasted_iota(jnp.int32, (chunk, chunk), 1)
    tri = jnp.where(c <= r, 1.0, 0.0).astype(BF16)
    row = lax.broadcasted_iota(jnp.int32, (sub, HEAD_DIM), 0)
    neg_masks = [jnp.where(row >= s, 0.0, -NEG_BIG).astype(F32) for s in range(sub)]
    blk_mask = None
    if nb > 1:
        width = sub * nb * (nb - 1) // 2
        rb = lax.broadcasted_iota(jnp.int32, (chunk, width), 0) // sub
        cc = lax.broadcasted_iota(jnp.int32, (chunk, width), 1)
        blk_mask = jnp.zeros((chunk, width), F32)
        off = 0
        for i in range(1, nb):
            hit = jnp.where(rb == i, jnp.where(cc >= off, jnp.where(cc < off + i * sub, 1.0, 0.0), 0.0), 0.0)
            blk_mask = blk_mask + hit
            off += i * sub
    return tri, blk_mask, neg_masks


def _cumsum_rows(tri_bf, g):
    g1 = g.astype(BF16)
    r1 = g - g1.astype(F32)
    g2 = r1.astype(BF16)
    g3 = (r1 - g2.astype(F32)).astype(BF16)
    return _dot(tri_bf, g1) + _dot(tri_bf, g2) + _dot(tri_bf, g3)


def _head_norm(o, gn):
    return o * lax.rsqrt(jnp.mean(o * o, axis=-1, keepdims=True) + LN_EPS) * gn


def _hgrn_prompt_kernel(q_ref, k_ref, v_ref, lf_ref, sg_ref, gn_ref, s0_ref, o_ref, s_ref,
                        st_ref, lc_ref, mild_ref, *, chunk, sub):
    tb = q_ref.shape[1]
    n_chunks = tb // chunk
    heads = q_ref.shape[2] // HEAD_DIM
    j = pl.program_id(2)
    consts = _hgrn_consts(chunk, sub)
    gn = gn_ref[...]

    @pl.when(j == 0)
    def _():
        for h in range(heads):
            st_ref[h] = s0_ref[0, h].T

    causal = (lax.broadcasted_iota(jnp.int32, (chunk, chunk), 1)
              <= lax.broadcasted_iota(jnp.int32, (chunk, chunk), 0))

    def prepare(n, slot):
        rows = pl.ds(pl.multiple_of(n * chunk, chunk), chunk)
        lcum = _cumsum_rows(consts[0], lf_ref[0, rows, :]) * LOG2_E
        lc_ref[slot] = lcum
        mild_ref[slot] = (jnp.min(lcum[chunk - 1:chunk, :]) >= -HGRN_MILD_LOG2).astype(jnp.int32)

    prepare(0, 0)

    def body(n, carry):
        rows = pl.ds(pl.multiple_of(n * chunk, chunk), chunk)
        slot = lax.rem(n, 2)
        mild = mild_ref[slot] == 1

        def run(step):
            for h in range(heads):
                lanes = slice(h * HEAD_DIM, (h + 1) * HEAD_DIM)
                o, st = step(q_ref[0, rows, lanes].astype(F32), k_ref[0, rows, lanes].astype(F32),
                             v_ref[0, rows, lanes].astype(F32), lc_ref[slot, :, lanes], st_ref[h])
                st_ref[h] = st
                o_ref[0, rows, lanes] = (_head_norm(o, gn) * sg_ref[0, rows, lanes].astype(F32)).astype(o_ref.dtype)
            prepare(jnp.minimum(n + 1, n_chunks - 1), 1 - slot)

        @pl.when(mild)
        def _():
            run(lambda q, k, v, g, st: _hgrn_chunk_mild(q, k, v, g, st, causal))

        @pl.when(jnp.logical_not(mild))
        def _():
            run(lambda q, k, v, g, st: _hgrn_chunk(q, k, v, g, st, consts, chunk, sub))

        return carry

    lax.fori_loop(0, n_chunks, body, 0)

    @pl.when(j == pl.num_programs(2) - 1)
    def _():
        for h in range(heads):
            s_ref[0, h] = st_ref[h].T


def _hgrn_prompt(q, k, v, lf, sg, gnorm, s0):
    bsz, t, d = q.shape
    tb = HGRN_BLOCK
    hg = HGRN_HEAD_GROUP
    blk = pl.BlockSpec((1, tb, hg * HEAD_DIM), lambda b, g, j: (b, j, g))
    sblk = pl.BlockSpec((1, hg, HEAD_DIM, HEAD_DIM), lambda b, g, j: (b, g, 0, 0))
    return pl.pallas_call(
        functools.partial(_hgrn_prompt_kernel, chunk=HGRN_CHUNK, sub=HGRN_SUB),
        grid=(bsz, HGRN_HEADS // hg, t // tb),
        in_specs=[blk, blk, blk, blk, blk, _full((1, HEAD_DIM)), sblk],
        out_specs=[blk, sblk],
        out_shape=[jax.ShapeDtypeStruct((bsz, t, d), BF16),
                   jax.ShapeDtypeStruct(s0.shape, F32)],
        scratch_shapes=[pltpu.VMEM((hg, HEAD_DIM, HEAD_DIM), F32),
                        pltpu.VMEM((2, HGRN_CHUNK, hg * HEAD_DIM), F32), pltpu.SMEM((2,), jnp.int32)],
        compiler_params=_params(("arbitrary", "arbitrary", "arbitrary")),
        name="hgrn_prompt",
    )(q, k, v, lf, sg, gnorm, s0)


def _hgrn_sample_kernel(q_ref, k_ref, v_ref, lf_ref, sg_ref, gn_ref, s0_ref, o_ref, s_ref):
    sb, t_new, d_model = q_ref.shape
    gn = gn_ref[...]
    pad_rows = SAMPLE_PAD - t_new - 1

    def body(b, carry):
        q, k, v, sg = q_ref[b], k_ref[b], v_ref[b], sg_ref[b]
        lf = lf_ref[b]
        lrow = []
        for t in range(t_new):
            lrow.append(lf[t:t + 1, :] if t == 0 else lrow[-1] + lf[t:t + 1, :])
        lcum = jnp.concatenate(lrow, axis=0)
        l_end = lrow[-1]
        qe = q * jnp.exp(lcum)
        ke = k * jnp.exp(l_end - lcum)
        f_end = jnp.exp(l_end)
        zeros = jnp.zeros((pad_rows, d_model), F32)
        zero_row = jnp.zeros((1, d_model), F32)
        qe_p = jnp.concatenate([qe, zero_row, zeros], axis=0)
        ke_p = jnp.concatenate([ke, f_end, zeros], axis=0)
        v_p = jnp.concatenate([v, zero_row, zeros], axis=0)
        for h in range(HGRN_HEADS):
            lanes = slice(h * HEAD_DIM, (h + 1) * HEAD_DIM)
            s0 = s0_ref[b, h]
            o = _dot(qe_p[:, lanes].astype(BF16), s0.astype(BF16))[0:t_new]
            rows = []
            for t in range(t_new):
                ot = None
                for s in range(t + 1):
                    prod = q[t:t + 1, lanes] * k[s:s + 1, lanes]
                    if s < t:
                        prod = prod * jnp.exp(lrow[t][:, lanes] - lrow[s][:, lanes])
                    term = jnp.sum(prod, axis=-1, keepdims=True) * v[s:s + 1, lanes]
                    ot = term if ot is None else ot + term
                rows.append(ot)
            o = o + jnp.concatenate(rows, axis=0)
            ke_t = ke_p[:, lanes].T
            decay = ke_t[:, t_new:t_new + 1]
            s_ref[b, h] = s0 * decay + _dot(ke_t.astype(BF16), v_p[:, lanes].astype(BF16))
            o_ref[b, :, lanes] = _head_norm(o, gn) * sg[:, lanes]
        return carry

    lax.fori_loop(0, sb, body, 0)


def _hgrn_sample(q, k, v, lf, sg, gnorm, s0):
    bsz, t, d = q.shape
    sb = HGRN_SAMPLE_BLOCK
    blk = pl.BlockSpec((sb, t, d), lambda g: (g, 0, 0))
    sblk = pl.BlockSpec((sb, HGRN_HEADS, HEAD_DIM, HEAD_DIM), lambda g: (g, 0, 0, 0))
    return pl.pallas_call(
        _hgrn_sample_kernel,
        grid=(bsz // sb,),
        in_specs=[blk, blk, blk, blk, blk, _full((1, HEAD_DIM)), sblk],
        out_specs=[blk, sblk],
        out_shape=[jax.ShapeDtypeStruct((bsz, t, d), F32),
                   jax.ShapeDtypeStruct(s0.shape, F32)],
        compiler_params=_params(("arbitrary",)),
        name="hgrn_sample",
    )(q, k, v, lf, sg, gnorm, s0)


def _merge_kernel(x_ref, ha_ref, hb_ref, ga_ref, gb_ref, wa_ref, wb_ref, wo_ref, g_ref, b_ref, o_ref):
    ya = _dot(ha_ref[...].astype(BF16), wa_ref[...])
    yb = _dot(hb_ref[...].astype(BF16), wb_ref[...])
    mixed = ga_ref[...].astype(F32) * ya + gb_ref[...].astype(F32) * yb
    z = DEEPNORM_ALPHA * x_ref[...] + _dot(mixed.astype(BF16), wo_ref[...])
    o_ref[...] = _layer_norm(z, g_ref[...], b_ref[...])


def _merge(x2d, ha, hb, ga, gb, wa, wb, wo, g, b):
    n, d = x2d.shape
    tm = MERGE_TOKEN_BLOCK if n % MERGE_TOKEN_BLOCK == 0 else TOKEN_BLOCK
    row = pl.BlockSpec((tm, d), lambda i: (i, 0))
    return pl.pallas_call(
        _merge_kernel,
        grid=(n // tm,),
        in_specs=[row] * 5 + [_full((d, d))] * 3 + [_full((1, d))] * 2,
        out_specs=row,
        out_shape=jax.ShapeDtypeStruct((n, d), F32),
        compiler_params=_params(("arbitrary",)),
        name="merge",
    )(x2d, ha, hb, ga, gb, wa, wb, wo, g, b)


def _route_t(s, bias):
    n = s.shape[1]
    neg = -jnp.inf
    sb = (s + bias).reshape(GROUP_SIZE, N_GROUPS, n)
    s3 = s.reshape(GROUP_SIZE, N_GROUPS, n)
    e_in_g = lax.broadcasted_iota(jnp.int32, sb.shape, 0)
    m1 = jnp.max(sb, axis=0, keepdims=True)
    first = jnp.min(jnp.where(sb == m1, e_in_g, GROUP_SIZE), axis=0, keepdims=True)
    m2 = jnp.max(jnp.where(e_in_g == first, neg, sb), axis=0, keepdims=True)
    gscore = (m1 + m2)[0]
    gid = lax.broadcasted_iota(jnp.int32, gscore.shape, 0)
    gsel = jnp.zeros(gscore.shape, F32)
    for _ in range(TOPK_GROUPS):
        gm = jnp.max(gscore, axis=0, keepdims=True)
        pick = jnp.min(jnp.where(gscore == gm, gid, N_GROUPS), axis=0, keepdims=True)
        hit = gid == pick
        gsel = jnp.where(hit, 1.0, gsel)
        gscore = jnp.where(hit, neg, gscore)
    cand = jnp.where(gsel[None, :, :] > 0.5, sb, neg)
    eid = lax.broadcasted_iota(jnp.int32, sb.shape, 1) * GROUP_SIZE + e_in_g
    esel = jnp.zeros(sb.shape, F32)
    for _ in range(TOP_K):
        em = jnp.max(jnp.max(cand, axis=0, keepdims=True), axis=1, keepdims=True)
        masked = jnp.where(cand == em, eid, N_EXPERTS)
        pick = jnp.min(jnp.min(masked, axis=0, keepdims=True), axis=1, keepdims=True)
        hit = eid == pick
        esel = jnp.where(hit, 1.0, esel)
        cand = jnp.where(hit, neg, cand)
    w = esel * s3
    tot = jnp.sum(jnp.sum(w, axis=0, keepdims=True), axis=1, keepdims=True)
    return w / tot * ROUTED_SCALE, jnp.max(esel, axis=0)


def _split_bf16(a):
    hi = a.astype(BF16)
    return hi, (a - hi.astype(F32)).astype(BF16)


def _moe_kernel(x_ref, p_ref, wrt_ref, rb_ref, wg_ref, wu_ref, wd_ref, sg_ref, su_ref, sd_ref,
                g2_ref, b2_ref, pg_ref, pp_ref, o_ref,
                xb_ref, gt_ref, acc_ref, key_ref, ok_ref, xg_ref, gg_ref, pt_ref, accg_ref):
    j = pl.program_id(1)
    eb = wg_ref.shape[0]
    tm = x_ref.shape[0]
    n_sub, cap = xg_ref.shape[0], xg_ref.shape[1]
    ts = tm // n_sub
    steps_per_group = GROUP_SIZE // eb
    group = j // steps_per_group
    step_in_group = j - group * steps_per_group

    @pl.when(j == 0)
    def _():
        x = x_ref[...]
        xb = x.astype(BF16)
        x_lo = (x - xb.astype(F32)).astype(BF16)
        w_hi, w_lo = _split_bf16(wrt_ref[...])
        logits = _dot_nt(w_hi, xb) + _dot_nt(w_hi, x_lo) + _dot_nt(w_lo, xb)
        gates, gsel = _route_t(_sigmoid(logits), rb_ref[...])
        for i in range(GROUP_SIZE):
            for g in range(N_GROUPS):
                gt_ref[g * GROUP_SIZE + i:g * GROUP_SIZE + i + 1, :] = gates[i, g:g + 1, :]
        earlier = (lax.broadcasted_iota(jnp.int32, (ts, ts), 0)
                   < lax.broadcasted_iota(jnp.int32, (ts, ts), 1))
        before = jnp.where(earlier, 1.0, 0.0).astype(BF16)
        for s in range(n_sub):
            sel = gsel[:, s * ts:(s + 1) * ts]
            key = jnp.where(sel > 0.5, _dot(sel.astype(BF16), before), -1.0)
            for g in range(N_GROUPS):
                key_ref[s, g] = key[g:g + 1, :]
        xb_ref[...] = xb
        hs = _silu(_dot(xb, sg_ref[...])) * _dot(xb, su_ref[...])
        acc_ref[...] = _dot(hs.astype(BF16), sd_ref[...])

    slab = pl.multiple_of(group * GROUP_SIZE, GROUP_SIZE)
    in_group = step_in_group * eb

    def experts(rows, gate_cols, add):
        for r in range(eb):
            gate = gate_cols[:, r:r + 1]
            for off in range(eb, GROUP_SIZE, eb):
                gate = jnp.where(in_group == off, gate_cols[:, off + r:off + r + 1], gate)
            h = _silu(_dot(rows, wg_ref[r])) * _dot(rows, wu_ref[r])
            add(_dot((h * gate).astype(BF16), wd_ref[r]))

    def add_compact(s):
        def add(y):
            accg_ref[s] += y
        return add

    def add_dense(tok):
        def add(y):
            acc_ref[tok, :] += y
        return add

    def add_all_compact(y):
        accg_ref[...] += y.reshape(accg_ref.shape)

    def compact_sub_block(s):
        tok = slice(s * ts, (s + 1) * ts)
        key_row = key_ref[s, group]
        slot = lax.broadcasted_iota(jnp.int32, (cap, ts), 0).astype(F32)
        pick = jnp.where(slot == key_row, 1.0, 0.0).astype(BF16)
        xg_ref[s] = _dot(pick, xb_ref[tok, :]).astype(BF16)
        g_rows = jnp.concatenate([gt_ref[pl.ds(slab, GROUP_SIZE), tok],
                                  jnp.zeros((LANES - GROUP_SIZE, ts), F32)], axis=0)
        g_hi, g_lo = _split_bf16(g_rows.T)
        gg_ref[s] = _dot(pick, g_hi) + _dot(pick, g_lo)
        key_col = jnp.broadcast_to(key_row, (SUBLANES, ts)).T[:, 0:1]
        slot_t = lax.broadcasted_iota(jnp.int32, (ts, cap), 1).astype(F32)
        pt_ref[s] = jnp.where(slot_t == key_col, 1.0, 0.0).astype(BF16)
        accg_ref[s] = jnp.zeros((cap, accg_ref.shape[2]), F32)

    @pl.when(step_in_group == 0)
    def _():
        fits = []
        for s in range(n_sub):
            count = jnp.sum(jnp.where(key_ref[s, group] >= 0.0, 1.0, 0.0))
            fits.append(count <= float(cap))
            ok_ref[s] = fits[s].astype(jnp.int32)
        all_fit = functools.reduce(jnp.logical_and, fits)

        @pl.when(all_fit)
        def _():
            for s in range(n_sub):
                compact_sub_block(s)

        for s in range(n_sub):
            @pl.when(jnp.logical_and(fits[s], jnp.logical_not(all_fit)))
            def _():
                compact_sub_block(s)

    flags = [ok_ref[s] == 1 for s in range(n_sub)]
    all_compact = functools.reduce(jnp.logical_and, flags)
    last_of_group = step_in_group == steps_per_group - 1

    def scatter_sub_block(s):
        tok = slice(s * ts, (s + 1) * ts)
        acc_ref[tok, :] += _dot(pt_ref[s], accg_ref[s].astype(BF16))

    @pl.when(all_compact)
    def _():
        experts(xg_ref[...].reshape(n_sub * cap, xg_ref.shape[2]),
                gg_ref[...].reshape(n_sub * cap, LANES), add_all_compact)

    for s in range(n_sub):
        tok = slice(s * ts, (s + 1) * ts)
        alone = jnp.logical_and(flags[s], jnp.logical_not(all_compact))

        @pl.when(alone)
        def _():
            experts(xg_ref[s], gg_ref[s], add_compact(s))

        @pl.when(jnp.logical_not(flags[s]))
        def _():
            experts(xb_ref[tok, :], gt_ref[pl.ds(slab, GROUP_SIZE), tok].T, add_dense(tok))

        @pl.when(jnp.logical_and(alone, last_of_group))
        def _():
            scatter_sub_block(s)

    @pl.when(jnp.logical_and(all_compact, last_of_group))
    def _():
        for s in range(n_sub):
            scatter_sub_block(s)

    @pl.when(j == pl.num_programs(1) - 1)
    def _():
        x2 = _layer_norm(DEEPNORM_ALPHA * x_ref[...] + acc_ref[...], g2_ref[...], b2_ref[...])
        gate = _sigmoid(_dot(x2.astype(BF16), pg_ref[...]))
        o_ref[...] = x2 + gate * _dot(p_ref[...].astype(BF16), pp_ref[...])


def _moe(x2d, p2d, wrt, rbias, wg, wu, wd, sg, su, sd, g2, b2, pg, pp):
    n, d = x2d.shape
    tm = MOE_TOKEN_BLOCK if n % MOE_TOKEN_BLOCK == 0 else MOE_SUB_BLOCK
    ts = MOE_SUB_BLOCK
    n_sub = tm // ts
    eb = EXPERTS_PER_STEP
    cap = MOE_GROUP_CAP
    assert GROUP_SIZE % eb == 0 and n % tm == 0 and tm % ts == 0
    dp = p2d.shape[1]
    ff = wg.shape[2]
    sff = sg.shape[1]
    row = pl.BlockSpec((tm, d), lambda i, j: (i, 0))
    return pl.pallas_call(
        _moe_kernel,
        grid=(n // tm, N_EXPERTS // eb),
        in_specs=[row, pl.BlockSpec((tm, dp), lambda i, j: (i, 0)),
                  _full((N_EXPERTS, d)), _full((N_EXPERTS, 1)),
                  pl.BlockSpec((eb, d, ff), lambda i, j: (j, 0, 0)),
                  pl.BlockSpec((eb, d, ff), lambda i, j: (j, 0, 0)),
                  pl.BlockSpec((eb, ff, d), lambda i, j: (j, 0, 0)),
                  _full((d, sff)), _full((d, sff)), _full((sff, d)),
                  _full((1, d)), _full((1, d)), _full((d, d)), _full((dp, d))],
        out_specs=row,
        out_shape=jax.ShapeDtypeStruct((n, d), F32),
        scratch_shapes=[pltpu.VMEM((tm, d), BF16), pltpu.VMEM((N_EXPERTS, tm), F32),
                        pltpu.VMEM((tm, d), F32),
                        pltpu.VMEM((n_sub, N_GROUPS, 1, ts), F32), pltpu.SMEM((n_sub,), jnp.int32),
                        pltpu.VMEM((n_sub, cap, d), BF16), pltpu.VMEM((n_sub, cap, LANES), F32),
                        pltpu.VMEM((n_sub, ts, cap), BF16), pltpu.VMEM((n_sub, cap, d), F32)],
        compiler_params=_params(("arbitrary", "arbitrary")),
        name="moe",
    )(x2d, p2d, wrt, rbias, wg, wu, wd, sg, su, sd, g2, b2, pg, pp)


def kernel(x_prompt, x_sample, p_prompt, p_sample, state_conv, state_hgrn, w_in, b_in, hgrn_lb, conv_w, conv_b, conv_ln_g, conv_ln_b, w_conv_out, hgrn_norm_g, w_hgrn_out, w_o, ln1_g, ln1_b, w_router, router_bias, w_exp_gate, w_exp_up, w_exp_down, w_sh_gate, w_sh_up, w_sh_down, ln2_g, ln2_b, w_ple_gate, w_ple_proj):
    assert w_in.shape[0] == DEPTH == 1
    bp, tp, d = x_prompt.shape
    bs, ts, _ = x_sample.shape
    nbuf = CONV_WIDTH - 1
    i = 0
    layer = lambda a: a.reshape(a.shape[1:])
    row = lambda a: a.reshape(1, -1)
    bf = lambda a: layer(a).astype(BF16)

    w_in_b, wco, who, wo = bf(w_in), bf(w_conv_out), bf(w_hgrn_out), bf(w_o)
    weg, weu, wed = bf(w_exp_gate), bf(w_exp_up), bf(w_exp_down)
    wsg, wsu, wsd = bf(w_sh_gate), bf(w_sh_up), bf(w_sh_down)
    wpg, wpp = bf(w_ple_gate), bf(w_ple_proj)
    regroup = lambda a: a.reshape(N_GROUPS, GROUP_SIZE, -1).swapaxes(0, 1).reshape(N_EXPERTS, -1)
    wrt = regroup(layer(w_router).T)
    rbias = regroup(router_bias.reshape(N_EXPERTS, 1))
    gnorm = row(hgrn_norm_g)

    def tail(x2d, p2d, ha, hb, ga, gb):
        x1 = _merge(x2d, ha, hb, ga, gb, wco, who, wo, row(ln1_g), row(ln1_b))
        return _moe(x1, p2d, wrt, rbias, weg, weu, wed, wsg, wsu, wsd, row(ln2_g), row(ln2_b), wpg, wpp)

    xp = x_prompt.reshape(bp * tp, d)
    ha, u_tail, q, k, v, lf, sg, ga, gb = _proj_conv(xp, tp, w_in_b, row(b_in), hgrn_lb, i, layer(conv_w),
                                                     row(conv_b), row(conv_ln_g), row(conv_ln_b))
    seq = lambda a: a.reshape(bp, tp, d)
    hb, hgrn_p = _hgrn_prompt(seq(q), seq(k), seq(v), seq(lf), seq(sg), gnorm,
                              jnp.zeros((bp, HGRN_HEADS, HEAD_DIM, HEAD_DIM), F32))
    y_p = tail(xp, p_prompt.reshape(bp * tp, -1), ha, hb.reshape(bp * tp, d), ga, gb)
    conv_p = u_tail.reshape(bp, CONV_HALO, d)[:, CONV_HALO - nbuf:, :]

    tmaj = lambda a: jnp.swapaxes(a, 0, 1).reshape(ts * bs, -1)
    xs = tmaj(x_sample)
    u, q, k, v, lf, sg, ga, gb = _proj(xs, w_in_b, row(b_in), hgrn_lb, i)
    seq = lambda a: a.reshape(ts, bs, d)
    ha, conv_s = _conv_sample(seq(u), layer(state_conv), layer(conv_w), row(conv_b), row(conv_ln_g),
                              row(conv_ln_b))
    bmaj = lambda a: jnp.swapaxes(seq(a), 0, 1).astype(F32)
    hb, hgrn_s = _hgrn_sample(bmaj(q), bmaj(k), bmaj(v), bmaj(lf), bmaj(sg), gnorm, layer(state_hgrn))
    y_s = tail(xs, tmaj(layer(p_sample)), ha.reshape(ts * bs, d), tmaj(hb), ga, gb)
    y_s = jnp.swapaxes(y_s.reshape(ts, bs, d), 0, 1)

    return (y_p.reshape(bp, tp, d), y_s, conv_p[None], hgrn_p[None], conv_s[None], hgrn_s[None])
```
